```python
import math
import jax, jax.numpy as jnp
from jax import lax
import numpy as np

D_MODEL = 2048
BATCH = 4
SEQ = 2048
DEPTH = 4
DEC_BATCH = 32
DEC_SEQ = 8
PAST_LEN = 16384
PAGE_SIZE = 128

HEAD_DIM = 128
ATTN_HEADS = 8
KV_HEADS = 2
GQA = ATTN_HEADS // KV_HEADS
ATTN_WIDTH = ATTN_HEADS * HEAD_DIM
KV_WIDTH = KV_HEADS * HEAD_DIM
WINDOW = 128
ATTN_BLOCK = WINDOW
DN_HEADS = 8
DN_DK = 128
DN_DV = 128
DN_KW = DN_HEADS * DN_DK
DN_VW = DN_HEADS * DN_DV
CONV_W = 4
CONV_CH = 2 * DN_KW + DN_VW
DN_CHUNK = 64
MIX_WIDTH = ATTN_WIDTH + DN_VW
IN_COLS = ATTN_WIDTH + 2 * KV_WIDTH + CONV_CH + DN_VW + 2 * DN_HEADS
N_GROUPS = 4
EXPERTS_PER_GROUP = 8
N_EXPERTS = N_GROUPS * EXPERTS_PER_GROUP
TOP_K = 2
D_FF_EXPERT = 256
DEEPNORM_ALPHA = (2 * DEPTH) ** 0.25
DEEPNORM_BETA = (8 * DEPTH) ** -0.25
LN_EPS = 1e-5
RMS_EPS = 1e-6
NEG_INF = -1e30

kernel_name = 'hymba_swa_gdn_hmoe_deepnorm_step'


def layer_norm(x, g, b):
    xf = x.astype(jnp.float32)
    mu = jnp.mean(xf, axis=-1, keepdims=True)
    var = jnp.mean(jnp.square(xf - mu), axis=-1, keepdims=True)
    y = (xf - mu) * lax.rsqrt(var + LN_EPS) * g.astype(jnp.float32) + b.astype(jnp.float32)
    return y.astype(x.dtype)


def alibi_slopes():
    return jnp.exp2(-8.0 * jnp.arange(1, ATTN_HEADS + 1, dtype=jnp.float32) / ATTN_HEADS)


def swa_attend(q, k, v, dist, valid, sinks):
    s = jnp.einsum('...qhgd,...khd->...hgqk', q.astype(jnp.float32), k.astype(jnp.float32)) * (HEAD_DIM ** -0.5)
    slopes = alibi_slopes().reshape(KV_HEADS, GQA)[:, :, None, None]
    s = s - slopes * dist[..., None, None, :, :]
    s = jnp.where(valid[..., None, None, :, :], s, NEG_INF)
    sink = jnp.broadcast_to(sinks.astype(jnp.float32).reshape(KV_HEADS, GQA)[:, :, None, None], s.shape[:-1] + (1,))
    p = jax.nn.softmax(jnp.concatenate([s, sink], axis=-1), axis=-1)[..., :-1]
    o = jnp.einsum('...hgqk,...khd->...qhgd', p, v.astype(jnp.float32))
    return o.astype(q.dtype)


def swa_prompt(q, k, v, sinks):
    B, T = q.shape[:2]
    nb = T // ATTN_BLOCK
    qb = q.reshape(B, nb, ATTN_BLOCK, KV_HEADS, GQA, HEAD_DIM)
    kb = k.reshape(B, nb, ATTN_BLOCK, KV_HEADS, HEAD_DIM)
    vb = v.reshape(B, nb, ATTN_BLOCK, KV_HEADS, HEAD_DIM)
    padspec = ((0, 0), (1, 0), (0, 0), (0, 0), (0, 0))
    kk = jnp.concatenate([jnp.pad(kb, padspec)[:, :-1], kb], axis=2)
    vv = jnp.concatenate([jnp.pad(vb, padspec)[:, :-1], vb], axis=2)
    qpos = jnp.arange(T, dtype=jnp.int32).reshape(nb, ATTN_BLOCK)
    kpos = qpos[:, :1] - ATTN_BLOCK + jnp.arange(2 * ATTN_BLOCK, dtype=jnp.int32)[None, :]
    dist = qpos[:, :, None] - kpos[:, None, :]
    valid = (kpos >= 0)[:, None, :] & (dist >= 0) & (dist <= WINDOW)
    o = swa_attend(qb, kk, vv, dist.astype(jnp.float32), valid, sinks)
    return o.reshape(B, T, ATTN_WIDTH)


def swa_sample(q, k, v, k_buf, v_buf, sinks):
    B, T = q.shape[:2]
    n_buf = k_buf.shape[1]
    kk = jnp.concatenate([k_buf.astype(k.dtype), k], axis=1)
    vv = jnp.concatenate([v_buf.astype(v.dtype), v], axis=1)
    qpos = PAST_LEN + jnp.arange(T, dtype=jnp.int32)
    kpos = jnp.concatenate([PAST_LEN - n_buf + jnp.arange(n_buf, dtype=jnp.int32), qpos])
    dist = qpos[:, None] - kpos[None, :]
    valid = (dist >= 0) & (dist <= WINDOW)
    o = swa_attend(q.reshape(B, T, KV_HEADS, GQA, HEAD_DIM), kk, vv, dist.astype(jnp.float32), valid, sinks)
    return o.reshape(B, T, ATTN_WIDTH), kk[:, -n_buf:], vv[:, -n_buf:]


def causal_conv(u, buf, w):
    T = u.shape[1]
    ext = jnp.concatenate([buf.astype(u.dtype), u], axis=1)
    out = ext[:, 0:T] * w[0]
    for i in range(1, CONV_W):
        out = out + ext[:, i:i + T] * w[i]
    return jax.nn.silu(out), ext[:, -(CONV_W - 1):]


def l2norm(x):
    xf = x.astype(jnp.float32)
    return xf * lax.rsqrt(jnp.sum(xf * xf, axis=-1, keepdims=True) + RMS_EPS)


def gated_delta_rule(q, k, v, g, beta, s0):
    B, T, H, _ = q.shape
    DV = v.shape[-1]
    C = min(DN_CHUNK, T)
    pad = (-T) % C
    N = (T + pad) // C

    def to_chunks(a):
        a = jnp.pad(a, [(0, 0), (0, pad)] + [(0, 0)] * (a.ndim - 2))
        a = a.reshape((B, N, C) + a.shape[2:])
        return jnp.moveaxis(a, (1, 3), (0, 2))

    qc, kc, vc, gc, bc = (to_chunks(a) for a in (q, k, v, g, beta))
    gcum = jnp.cumsum(gc, axis=-1)
    idx = jnp.arange(C)
    incl = idx[:, None] >= idx[None, :]
    strict = idx[:, None] > idx[None, :]
    decay = jnp.where(incl, jnp.exp(jnp.where(incl, gcum[..., :, None] - gcum[..., None, :], 0.0)), 0.0)
    kb = kc * bc[..., None]
    A = jnp.where(strict, jnp.einsum('nbhid,nbhjd->nbhij', kb, kc) * decay, 0.0)
    rhs = jnp.concatenate([vc * bc[..., None], kb * jnp.exp(gcum)[..., None]], axis=-1)
    eye_a = jnp.broadcast_to(jnp.eye(C, dtype=A.dtype), A.shape) + A
    sol = lax.linalg.triangular_solve(eye_a, rhs, left_side=True, lower=True, unit_diagonal=True)
    u, w = sol[..., :DV], sol[..., DV:]
    qk = jnp.where(incl, jnp.einsum('nbhid,nbhjd->nbhij', qc, kc) * decay, 0.0)
    q_dec = qc * jnp.exp(gcum)[..., None]
    k_dec = kc * jnp.exp(gcum[..., -1:] - gcum)[..., None]
    g_last = jnp.exp(gcum[..., -1])

    def step(S, xs):
        u_i, w_i, qk_i, q_i, k_i, gl_i = xs
        v_new = u_i - jnp.einsum('bhck,bhkv->bhcv', w_i, S)
        o = jnp.einsum('bhck,bhkv->bhcv', q_i, S) + jnp.einsum('bhij,bhjv->bhiv', qk_i, v_new)
        S = S * gl_i[..., None, None] + jnp.einsum('bhck,bhcv->bhkv', k_i, v_new)
        return S, o

    s_final, o = lax.scan(step, s0, (u, w, qk, q_dec, k_dec, g_last))
    o = jnp.moveaxis(o, (0, 2), (1, 3)).reshape(B, N * C, H, DV)[:, :T]
    return o, s_final


def token_mixers(x, k_buf, v_buf, conv_buf, s0, w_in, conv_w, a_log, dt_bias, dn_norm_w, attn_sinks, w_out):
    B, T, _ = x.shape
    proj = jnp.einsum('btd,dc->btc', x, w_in)
    offs = np.cumsum([ATTN_WIDTH, KV_WIDTH, KV_WIDTH, CONV_CH, DN_VW]).tolist()
    aq, ak, av, dn_qkv, dn_z, dn_ab = jnp.split(proj, offs, axis=-1)
    aq = aq.reshape(B, T, KV_HEADS, GQA, HEAD_DIM)
    ak = ak.reshape(B, T, KV_HEADS, HEAD_DIM)
    av = av.reshape(B, T, KV_HEADS, HEAD_DIM)
    if k_buf is None:
        o_attn = swa_prompt(aq, ak, av, attn_sinks)
        k_new, v_new = ak[:, -WINDOW:], av[:, -WINDOW:]
        conv_buf = jnp.zeros((B, CONV_W - 1, CONV_CH), x.dtype)
        s0 = jnp.zeros((B, DN_HEADS, DN_DK, DN_DV), jnp.float32)
    else:
        o_attn, k_new, v_new = swa_sample(aq, ak, av, k_buf, v_buf, attn_sinks)
    dn_qkv, conv_new = causal_conv(dn_qkv, conv_buf, conv_w)
    dq, dk, dv = jnp.split(dn_qkv, [DN_KW, 2 * DN_KW], axis=-1)
    dq = l2norm(dq.reshape(B, T, DN_HEADS, DN_DK)) * (DN_DK ** -0.5)
    dk = l2norm(dk.reshape(B, T, DN_HEADS, DN_DK))
    dv = dv.reshape(B, T, DN_HEADS, DN_DV).astype(jnp.float32)
    a_in = dn_ab[..., :DN_HEADS].astype(jnp.float32)
    b_in = dn_ab[..., DN_HEADS:].astype(jnp.float32)
    g = -jnp.exp(a_log.astype(jnp.float32)) * jax.nn.softplus(a_in + dt_bias.astype(jnp.float32))
    beta = jax.nn.sigmoid(b_in)
    o_dn, s_new = gated_delta_rule(dq, dk, dv, g, beta, s0.astype(jnp.float32))
    z = dn_z.reshape(B, T, DN_HEADS, DN_DV).astype(jnp.float32)
    o_dn = o_dn * lax.rsqrt(jnp.mean(o_dn * o_dn, axis=-1, keepdims=True) + RMS_EPS)
    o_dn = o_dn * dn_norm_w.astype(jnp.float32) * jax.nn.silu(z)
    mix = jnp.concatenate([o_attn, o_dn.reshape(B, T, DN_VW).astype(x.dtype)], axis=-1)
    out = jnp.einsum('btm,md->btd', mix, w_out)
    return out, k_new, v_new, conv_new, s_new


def hier_moe(x, wg, bg, we, be, w_gu, w_dn):
    B, T, D = x.shape
    xf = x.reshape(B * T, D)
    lg = jnp.einsum('nd,dg->ng', xf, wg).astype(jnp.float32) + bg.astype(jnp.float32)
    pg = jax.nn.softmax(lg, axis=-1)
    grp = jnp.argmax(lg, axis=-1)
    gate_group = jnp.max(pg, axis=-1)
    le = jnp.einsum('nd,de->ne', xf, we).astype(jnp.float32) + be.astype(jnp.float32)
    le = le.reshape(B * T, N_GROUPS, EXPERTS_PER_GROUP)
    le_sel = jnp.einsum('nge,ng->ne', le, jax.nn.one_hot(grp, N_GROUPS, dtype=jnp.float32))
    top_v, top_i = lax.top_k(le_sel, TOP_K)
    gates = jax.nn.softmax(top_v, axis=-1) * gate_group[:, None]
    eid = grp[:, None] * EXPERTS_PER_GROUP + top_i
    dense_gate = jnp.einsum('nk,nke->ne', gates, jax.nn.one_hot(eid, N_EXPERTS, dtype=jnp.float32))
    h = jnp.einsum('nd,edf->nef', xf, w_gu)
    hg, hu = jnp.split(h, 2, axis=-1)
    act = jax.nn.silu(hg) * hu * dense_gate[:, :, None].astype(x.dtype)
    y = jnp.einsum('nef,efd->nd', act, w_dn)
    return y.reshape(B, T, D)


def decoder_layer(x, k_buf, v_buf, conv_buf, s0, w_in, conv_w, a_log, dt_bias, dn_norm_w, attn_sinks, w_out,
                  ln1_g, ln1_b, rg_w, rg_b, re_w, re_b, w_gu, w_dn, ln2_g, ln2_b):
    mix, k_new, v_new, c_new, s_new = token_mixers(x, k_buf, v_buf, conv_buf, s0, w_in, conv_w, a_log, dt_bias,
                                                   dn_norm_w, attn_sinks, w_out)
    x = layer_norm(DEEPNORM_ALPHA * x + mix, ln1_g, ln1_b)
    x = layer_norm(DEEPNORM_ALPHA * x + hier_moe(x, rg_w, rg_b, re_w, re_b, w_gu, w_dn), ln2_g, ln2_b)
    return x, k_new, v_new, c_new, s_new


def setup_inputs(seed: int = 0) -> dict:
    key = jax.random.key(seed)
    ks = jax.random.split(key, 32)
    f32 = jnp.float32

    def nrm(k, shape, scale):
        return jax.random.normal(k, shape, f32) * scale

    n_buf = min(WINDOW, PAST_LEN)
    dt = jnp.exp(jax.random.uniform(ks[10], (DEPTH, DN_HEADS), f32, minval=math.log(1e-3), maxval=math.log(1e-1)))
    return {
        'x_prompt': nrm(ks[0], (BATCH, SEQ, D_MODEL), 1.0),
        'x_sample': nrm(ks[1], (DEC_BATCH, DEC_SEQ, D_MODEL), 1.0),
        'cache_swa_k': nrm(ks[2], (DEPTH, DEC_BATCH, n_buf, KV_HEADS, HEAD_DIM), 1.0),
        'cache_swa_v': nrm(ks[3], (DEPTH, DEC_BATCH, n_buf, KV_HEADS, HEAD_DIM), 1.0),
        'state_conv': nrm(ks[4], (DEPTH, DEC_BATCH, CONV_W - 1, CONV_CH), 1.0),
        'state_delta': nrm(ks[5], (DEPTH, DEC_BATCH, DN_HEADS, DN_DK, DN_DV), 0.1),
        'ln_in_g': 1.0 + nrm(ks[6], (D_MODEL,), 0.02),
        'ln_in_b': nrm(ks[7], (D_MODEL,), 0.02),
        'w_in': nrm(ks[8], (DEPTH, D_MODEL, IN_COLS), D_MODEL ** -0.5),
        'conv_w': nrm(ks[9], (DEPTH, CONV_W, CONV_CH), CONV_W ** -0.5),
        'a_log': jnp.log(jax.random.uniform(ks[11], (DEPTH, DN_HEADS), f32, minval=1.0, maxval=16.0)),
        'dt_bias': dt + jnp.log(-jnp.expm1(-dt)),
        'dn_norm_w': 1.0 + nrm(ks[12], (DEPTH, DN_DV), 0.02),
        'attn_sinks': nrm(ks[13], (DEPTH, ATTN_HEADS), 0.5),
        'w_out': nrm(ks[14], (DEPTH, MIX_WIDTH, D_MODEL), DEEPNORM_BETA * MIX_WIDTH ** -0.5),
        'ln1_g': 1.0 + nrm(ks[15], (DEPTH, D_MODEL), 0.02),
        'ln1_b': nrm(ks[16], (DEPTH, D_MODEL), 0.02),
        'router_group_w': nrm(ks[17], (DEPTH, D_MODEL, N_GROUPS), D_MODEL ** -0.5),
        'router_group_b': nrm(ks[18], (DEPTH, N_GROUPS), 0.01),
        'router_expert_w': nrm(ks[19], (DEPTH, D_MODEL, N_EXPERTS), D_MODEL ** -0.5),
        'router_expert_b': nrm(ks[20], (DEPTH, N_EXPERTS), 0.01),
        'w_gate_up': nrm(ks[21], (DEPTH, N_EXPERTS, D_MODEL, 2 * D_FF_EXPERT), D_MODEL ** -0.5),
        'w_down': nrm(ks[22], (DEPTH, N_EXPERTS, D_FF_EXPERT, D_MODEL), DEEPNORM_BETA * D_FF_EXPERT ** -0.5),
        'ln2_g': 1.0 + nrm(ks[23], (DEPTH, D_MODEL), 0.02),
        'ln2_b': nrm(ks[24], (DEPTH, D_MODEL), 0.02),
    }


def reference(x_prompt, x_sample, cache_swa_k, cache_swa_v, state_conv, state_delta,
              ln_in_g, ln_in_b, w_in, conv_w, a_log, dt_bias, dn_norm_w, attn_sinks, w_out,
              ln1_g, ln1_b, router_group_w, router_group_b, router_expert_w, router_expert_b,
              w_gate_up, w_down, ln2_g, ln2_b):
    xp = layer_norm(x_prompt, ln_in_g, ln_in_b)
    xs = layer_norm(x_sample, ln_in_g, ln_in_b)
    kp, vp, cp, sp = [], [], [], []
    kq, vq, cq, sq = [], [], [], []
    for l in range(DEPTH):
        params = (w_in[l], conv_w[l], a_log[l], dt_bias[l], dn_norm_w[l], attn_sinks[l], w_out[l],
                  ln1_g[l], ln1_b[l], router_group_w[l], router_group_b[l], router_expert_w[l],
                  router_expert_b[l], w_gate_up[l], w_down[l], ln2_g[l], ln2_b[l])
        xp, k1, v1, c1, s1 = decoder_layer(xp, None, None, None, None, *params)
        xs, k2, v2, c2, s2 = decoder_layer(xs, cache_swa_k[l], cache_swa_v[l], state_conv[l], state_delta[l], *params)
        kp.append(k1); vp.append(v1); cp.append(c1); sp.append(s1)
        kq.append(k2); vq.append(v2); cq.append(c2); sq.append(s2)
    return (xp, xs,
            jnp.stack(kp), jnp.stack(vp), jnp.stack(cp), jnp.stack(sp),
            jnp.stack(kq), jnp.stack(vq), jnp.stack(cq), jnp.stack(sq))
```

```python
import functools

import jax
import jax.numpy as jnp
from jax import lax
from jax.experimental import pallas as pl
from jax.experimental.pallas import tpu as pltpu

D_MODEL = 2048
BATCH = 4
SEQ = 2048
DEPTH = 4
DEC_BATCH = 32
DEC_SEQ = 8
HEAD_DIM = 128
ATTN_HEADS = 8
KV_HEADS = 2
GQA = ATTN_HEADS // KV_HEADS
ATTN_WIDTH = ATTN_HEADS * HEAD_DIM
KV_WIDTH = KV_HEADS * HEAD_DIM
WINDOW = 128
DN_HEADS = 8
DN_DK = 128
DN_DV = 128
DN_KW = DN_HEADS * DN_DK
DN_VW = DN_HEADS * DN_DV
CONV_W = 4
CONV_CH = 2 * DN_KW + DN_VW
MIX_WIDTH = ATTN_WIDTH + DN_VW
IN_COLS = ATTN_WIDTH + 2 * KV_WIDTH + CONV_CH + DN_VW + 2 * DN_HEADS
N_GROUPS = 4
EXPERTS_PER_GROUP = 8
N_EXPERTS = N_GROUPS * EXPERTS_PER_GROUP
D_FF_EXPERT = 256
DEEPNORM_ALPHA = (2 * DEPTH) ** 0.25
LN_EPS = 1e-5
RMS_EPS = 1e-6
NEG_INF = -1e30

N_PROMPT = BATCH * SEQ
N_SAMPLE = DEC_BATCH * DEC_SEQ
N_TOK = N_PROMPT + N_SAMPLE
N_ASSIGN = 2 * N_TOK

COL_K = ATTN_WIDTH
COL_V = COL_K + KV_WIDTH
COL_QKV = COL_V + KV_WIDTH
COL_Z = COL_QKV + CONV_CH
COL_AB = COL_Z + DN_VW

LANES = 128
ROW_TILE = 256
PROJ_TM = 768
PROJ_TN = 512
GDN_CHUNK = 64
GDN_COLS = 1536
SAMPLE_BB = 4
MOE_TM = 256
N_MOE_TILES = N_ASSIGN // MOE_TM
N_MOE_ITEMS = N_MOE_TILES + N_EXPERTS - 1
VMEM_LIMIT = 48 * 1024 * 1024

F32 = jnp.float32
BF16 = jnp.bfloat16


def _params(*sem):
    return pltpu.CompilerParams(dimension_semantics=sem, vmem_limit_bytes=VMEM_LIMIT)


def _mm(a, b):
    return jnp.dot(a.astype(BF16), b.astype(BF16), preferred_element_type=F32)


def _mm_nt(a, b):
    return lax.dot_general(a.astype(BF16), b.astype(BF16), (((1,), (1,)), ((), ())),
                           preferred_element_type=F32)


def _mm_tn(a, b):
    return lax.dot_general(a.astype(BF16), b.astype(BF16), (((0,), (0,)), ((), ())),
                           preferred_element_type=F32)


def _layer_norm(x, g, b):
    mu = jnp.mean(x, axis=-1, keepdims=True)
    xc = x - mu
    var = jnp.mean(xc * xc, axis=-1, keepdims=True)
    return xc * lax.rsqrt(var + LN_EPS) * g + b


def _silu(x):
    return x / (1.0 + jnp.exp(-x))


def _ln_in_kernel(x_ref, g_ref, b_ref, o_ref, ob_ref):
    y = _layer_norm(x_ref[...], g_ref[...], b_ref[...])
    o_ref[...] = y
    ob_ref[...] = y.astype(BF16)


def _ln_in(x, g, b):
    row = pl.BlockSpec((ROW_TILE, D_MODEL), lambda i: (i, 0))
    vec = pl.BlockSpec((1, D_MODEL), lambda i: (0, 0))
    return pl.pallas_call(
        _ln_in_kernel,
        grid=(N_TOK // ROW_TILE,),
        in_specs=[row, vec, vec],
        out_specs=[row, row],
        out_shape=[jax.ShapeDtypeStruct((N_TOK, D_MODEL), F32),
                   jax.ShapeDtypeStruct((N_TOK, D_MODEL), BF16)],
        compiler_params=_params("arbitrary"),
        name="ln_in",
    )(x, g.reshape(1, D_MODEL), b.reshape(1, D_MODEL))


def _proj_kernel(x_ref, w_ref, o_ref, wb_ref):
    @pl.when(pl.program_id(1) == 0)
    def _():
        wb_ref[...] = w_ref[...].astype(BF16)

    o_ref[...] = jnp.dot(x_ref[...], wb_ref[...], preferred_element_type=F32)


def _proj(xb, w_in):
    return pl.pallas_call(
        _proj_kernel,
        grid=(pl.cdiv(IN_COLS, PROJ_TN), N_TOK // PROJ_TM),
        in_specs=[pl.BlockSpec((PROJ_TM, D_MODEL), lambda j, i: (i, 0)),
                  pl.BlockSpec((D_MODEL, PROJ_TN), lambda j, i: (0, j))],
        out_specs=pl.BlockSpec((PROJ_TM, PROJ_TN), lambda j, i: (i, j)),
        out_shape=jax.ShapeDtypeStruct((N_TOK, IN_COLS), F32),
        scratch_shapes=[pltpu.VMEM((D_MODEL, PROJ_TN), BF16)],
        compiler_params=_params("arbitrary", "arbitrary"),
        name="in_proj",
    )(xb, w_in)


def _head_column(values, rows_per_head, n_rows):
    grp = lax.broadcasted_iota(jnp.int32, (n_rows, 1), 0) // rows_per_head
    col = jnp.full((n_rows, 1), values[-1], F32)
    for g in range(len(values) - 2, -1, -1):
        col = jnp.where(grp == g, values[g], col)
    return col


def _softmax_with_sink(s, sink_col):
    m = jnp.maximum(jnp.max(s, axis=-1, keepdims=True), sink_col)
    p = jnp.exp(s - m)
    denom = jnp.sum(p, axis=-1, keepdims=True) + jnp.exp(sink_col - m)
    return p, denom


def _attn_prompt_kernel(sink_ref, q_ref, kc_ref, kp_ref, vc_ref, vp_ref, o_ref):
    blk = pl.program_id(1)
    n_rows = GQA * WINDOW
    qpos = lax.broadcasted_iota(jnp.int32, (n_rows, 2 * WINDOW), 0) % WINDOW
    kidx = lax.broadcasted_iota(jnp.int32, (n_rows, 2 * WINDOW), 1)
    dist_i = qpos + WINDOW - kidx
    valid = (dist_i >= 0) & (dist_i <= WINDOW) & ((kidx >= WINDOW) | (blk > 0))
    dist = dist_i.astype(F32)
    for kvh in range(KV_HEADS):
        cols = slice(kvh * HEAD_DIM, (kvh + 1) * HEAD_DIM)
        kk = jnp.concatenate([kp_ref[:, cols], kc_ref[:, cols]], axis=0)
        vv = jnp.concatenate([vp_ref[:, cols], vc_ref[:, cols]], axis=0)
        heads = [kvh * GQA + g for g in range(GQA)]
        qs = jnp.concatenate([q_ref[:, h * HEAD_DIM:(h + 1) * HEAD_DIM] for h in heads], axis=0)
        slope = _head_column([2.0 ** (-8.0 * (h + 1) / ATTN_HEADS) for h in heads], WINDOW, n_rows)
        sink = _head_column([sink_ref[h] for h in heads], WINDOW, n_rows)
        s = _mm_nt(qs, kk) * (HEAD_DIM ** -0.5) - slope * dist
        s = jnp.where(valid, s, NEG_INF)
        p, denom = _softmax_with_sink(s, sink)
        o = _mm(p, vv) / denom
        for g, h in enumerate(heads):
            o_ref[:, h * HEAD_DIM:(h + 1) * HEAD_DIM] = o[g * WINDOW:(g + 1) * WINDOW].astype(BF16)


def _attn_prompt(proj, sinks):
    nb = SEQ // WINDOW
    kcol, vcol = COL_K // KV_WIDTH, COL_V // KV_WIDTH

    def cur(c):
        return lambda b, i: (b * nb + i, c)

    def prev(c):
        return lambda b, i: (jnp.maximum(b * nb + i - 1, 0), c)

    return pl.pallas_call(
        _attn_prompt_kernel,
        grid=(BATCH, nb),
        in_specs=[pl.BlockSpec(memory_space=pltpu.SMEM),
                  pl.BlockSpec((WINDOW, ATTN_WIDTH), cur(0)),
                  pl.BlockSpec((WINDOW, KV_WIDTH), cur(kcol)),
                  pl.BlockSpec((WINDOW, KV_WIDTH), prev(kcol)),
                  pl.BlockSpec((WINDOW, KV_WIDTH), cur(vcol)),
                  pl.BlockSpec((WINDOW, KV_WIDTH), prev(vcol))],
        out_specs=pl.BlockSpec((WINDOW, ATTN_WIDTH), lambda b, i: (b * nb + i, 0)),
        out_shape=jax.ShapeDtypeStruct((N_PROMPT, ATTN_WIDTH), BF16),
        compiler_params=_params("arbitrary", "arbitrary"),
        name="attn_prompt",
    )(sinks, proj, proj, proj, proj, proj)


def _attn_sample_kernel(sink_ref, q_ref, kn_ref, vn_ref, kb_ref, vb_ref, o_ref, ko_ref, vo_ref):
    n_rows = GQA * DEC_SEQ
    tpos = lax.broadcasted_iota(jnp.int32, (n_rows, 2 * WINDOW), 0) % DEC_SEQ
    kidx = lax.broadcasted_iota(jnp.int32, (n_rows, 2 * WINDOW), 1)
    dist_i = jnp.where(kidx < WINDOW, tpos + WINDOW - kidx, tpos - (kidx - WINDOW))
    valid = (dist_i >= 0) & (dist_i <= WINDOW) & (kidx < WINDOW + DEC_SEQ)
    dist = dist_i.astype(F32)
    pad = jnp.zeros((WINDOW - DEC_SEQ, HEAD_DIM), F32)
    for bb in range(SAMPLE_BB):
        rows = slice(bb * DEC_SEQ, (bb + 1) * DEC_SEQ)
        for kvh in range(KV_HEADS):
            cols = slice(kvh * HEAD_DIM, (kvh + 1) * HEAD_DIM)
            kk = jnp.concatenate([kb_ref[bb, :, cols], kn_ref[rows, cols], pad], axis=0)
            vv = jnp.concatenate([vb_ref[bb, :, cols], vn_ref[rows, cols], pad], axis=0)
            heads = [kvh * GQA + g for g in range(GQA)]
            qs = jnp.concatenate([q_ref[rows, h * HEAD_DIM:(h + 1) * HEAD_DIM] for h in heads], axis=0)
            slope = _head_column([2.0 ** (-8.0 * (h + 1) / ATTN_HEADS) for h in heads], DEC_SEQ, n_rows)
            sink = _head_column([sink_ref[h] for h in heads], DEC_SEQ, n_rows)
            s = _mm_nt(qs, kk) * (HEAD_DIM ** -0.5) - slope * dist
            s = jnp.where(valid, s, NEG_INF)
            p, denom = _softmax_with_sink(s, sink)
            o = _mm(p, vv) / denom
            for g, h in enumerate(heads):
                o_ref[rows, h * HEAD_DIM:(h + 1) * HEAD_DIM] = o[g * DEC_SEQ:(g + 1) * DEC_SEQ]
        ko_ref[bb, 0:WINDOW - DEC_SEQ, :] = kb_ref[bb, DEC_SEQ:WINDOW, :]
        ko_ref[bb, WINDOW - DEC_SEQ:WINDOW, :] = kn_ref[rows, :]
        vo_ref[bb, 0:WINDOW - DEC_SEQ, :] = vb_ref[bb, DEC_SEQ:WINDOW, :]
        vo_ref[bb, WINDOW - DEC_SEQ:WINDOW, :] = vn_ref[rows, :]


def _attn_sample(proj, sinks, k_buf, v_buf):
    rows = SAMPLE_BB * DEC_SEQ
    row0 = N_PROMPT // rows
    kcol, vcol = COL_K // KV_WIDTH, COL_V // KV_WIDTH
    cache = pl.BlockSpec((SAMPLE_BB, WINDOW, KV_WIDTH), lambda i: (i, 0, 0))
    cache_shape = jax.ShapeDtypeStruct((DEC_BATCH, WINDOW, KV_WIDTH), F32)
    return pl.pallas_call(
        _attn_sample_kernel,
        grid=(DEC_BATCH // SAMPLE_BB,),
        in_specs=[pl.BlockSpec(memory_space=pltpu.SMEM),
                  pl.BlockSpec((rows, ATTN_WIDTH), lambda i: (row0 + i, 0)),
                  pl.BlockSpec((rows, KV_WIDTH), lambda i: (row0 + i, kcol)),
                  pl.BlockSpec((rows, KV_WIDTH), lambda i: (row0 + i, vcol)),
                  cache, cache],
        out_specs=[pl.BlockSpec((rows, ATTN_WIDTH), lambda i: (i, 0)), cache, cache],
        out_shape=[jax.ShapeDtypeStruct((N_SAMPLE, ATTN_WIDTH), F32), cache_shape, cache_shape],
        compiler_params=_params("arbitrary"),
        name="attn_sample",
    )(sinks, proj, proj, proj, k_buf, v_buf)


def _cumsum_rows(x):
    n = x.shape[0]
    row = lax.broadcasted_iota(jnp.int32, x.shape, 0)
    step = 1
    while step < n:
        x = x + jnp.where(row >= step, pltpu.roll(x, step, axis=0), 0.0)
        step *= 2
    return x


def _unit_lower_inverse(a):
    n = a.shape[0]
    eye = (lax.broadcasted_iota(jnp.int32, (n, n), 0) == lax.broadcasted_iota(jnp.int32, (n, n), 1)).astype(F32)
    p = -a
    inv = eye + p
    span = 2
    while span < n:
        p = _mm(p, p)
        inv = inv + _mm(inv, p)
        span *= 2
    return inv


def _gdn_kernel(*refs, chunk, rows_in, has_state):
    if has_state:
        b1_ref, b2_ref, b3_ref, cw_ref, gp_ref, nw_ref, st_ref, s0_ref, o_ref, s_ref, ext_ref = refs
        first = True
    else:
        b1_ref, b2_ref, b3_ref, cw_ref, gp_ref, nw_ref, o_ref, s_ref, ext_ref = refs
        first = pl.program_id(1) == 0

    if has_state:
        ext_ref[0:8, :] = st_ref[...]
        s_ref[...] = s0_ref[...]
    else:
        @pl.when(first)
        def _():
            ext_ref[0:8, :] = jnp.zeros((8, CONV_CH), F32)
            s_ref[...] = jnp.zeros_like(s_ref)

        @pl.when(jnp.logical_not(first))
        def _():
            ext_ref[0:8, :] = ext_ref[chunk:chunk + 8, :]

    ext_ref[8:8 + rows_in, 0:GDN_COLS] = b1_ref[...]
    ext_ref[8:8 + rows_in, GDN_COLS:CONV_CH] = b2_ref[...]
    if rows_in < chunk:
        ext_ref[8 + rows_in:8 + chunk, :] = jnp.zeros((chunk - rows_in, CONV_CH), F32)

    lane = lax.broadcasted_iota(jnp.int32, (chunk, LANES), 1)
    row = lax.broadcasted_iota(jnp.int32, (chunk, LANES), 0)
    if rows_in < chunk:
        ab = jnp.concatenate([b3_ref[:, DN_VW:DN_VW + LANES],
                              jnp.zeros((chunk - rows_in, LANES), F32)], axis=0)
    else:
        ab = b3_ref[:, DN_VW:DN_VW + LANES]
    live = (lane < 2 * DN_HEADS) & (row < rows_in)
    ab = jnp.where(live, ab, 0.0)
    a_scale = gp_ref[0:1, :]
    dt_bias = gp_ref[1:2, :]
    g_all = jnp.where(live, a_scale * jax.nn.softplus(ab + dt_bias), 0.0)
    beta_all = jnp.where(live, jax.nn.sigmoid(ab), 0.0)
    gcum = _cumsum_rows(g_all)
    gcum_t = gcum.T
    g_last_all = gcum[chunk - 1:chunk, :]

    ri = lax.broadcasted_iota(jnp.int32, (chunk, chunk), 0)
    ci = lax.broadcasted_iota(jnp.int32, (chunk, chunk), 1)
    incl = ri >= ci
    strict = ri > ci

    def conv_strip(col0):
        cols = slice(col0, col0 + LANES)
        acc = ext_ref[8:8 + chunk, cols] * cw_ref[CONV_W - 1:CONV_W, cols]
        for tap in range(1, CONV_W):
            acc = acc + ext_ref[8 - tap:8 - tap + chunk, cols] * cw_ref[CONV_W - 1 - tap:CONV_W - tap, cols]
        return _silu(acc)

    def l2norm(x):
        return x * lax.rsqrt(jnp.sum(x * x, axis=-1, keepdims=True) + RMS_EPS)

    for h in range(DN_HEADS):
        q = l2norm(conv_strip(h * DN_DK)) * (DN_DK ** -0.5)
        k = l2norm(conv_strip(DN_KW + h * DN_DK))
        v = conv_strip(2 * DN_KW + h * DN_DV)
        gc_col = gcum[:, h:h + 1]
        gc_row = gcum_t[h:h + 1, :]
        g_last = g_last_all[:, h:h + 1]
        beta = beta_all[:, DN_HEADS + h:DN_HEADS + h + 1]
        decay = jnp.where(incl, jnp.exp(jnp.where(incl, gc_col - gc_row, 0.0)), 0.0)
        eg = jnp.exp(gc_col)
        kb = k * beta
        a = jnp.where(strict, _mm_nt(kb, k) * decay, 0.0)
        qk = jnp.where(incl, _mm_nt(q, k) * decay, 0.0)
        sol = _mm(_unit_lower_inverse(a), jnp.concatenate([v * beta, kb * eg], axis=1))
        u, w = sol[:, :DN_DV], sol[:, DN_DV:]
        s_prev = s_ref[h]
        v_new = u - _mm(w, s_prev)
        o = _mm(q * eg, s_prev) + _mm(qk, v_new)
        k_dec = k * jnp.exp(g_last - gc_col)
        s_ref[h] = s_prev * jnp.exp(g_last) + _mm_tn(k_dec, v_new)
        o = o * lax.rsqrt(jnp.mean(o * o, axis=-1, keepdims=True) + RMS_EPS)
        z = b3_ref[:, h * DN_DV:(h + 1) * DN_DV]
        out = o[0:rows_in] * nw_ref[...] * _silu(z)
        o_ref[:, h * DN_DV:(h + 1) * DN_DV] = out.astype(o_ref.dtype)


def _gdn_gate_params(a_log, dt_bias):
    gp = jnp.zeros((2, LANES), F32)
    gp = gp.at[0, :DN_HEADS].set(-jnp.exp(a_log.astype(F32)))
    return gp.at[1, :DN_HEADS].set(dt_bias.astype(F32))


def _gdn_prompt(proj, conv_w, gp, norm_w):
    nc = SEQ // GDN_CHUNK
    c0 = COL_QKV // GDN_COLS

    def win(c):
        return pl.BlockSpec((GDN_CHUNK, GDN_COLS), lambda b, i: (b * nc + i, c))

    full = lambda shape: pl.BlockSpec(shape, lambda b, i: (0,) * len(shape))
    return pl.pallas_call(
        functools.partial(_gdn_kernel, chunk=GDN_CHUNK, rows_in=GDN_CHUNK, has_state=False),
        grid=(BATCH, nc),
        in_specs=[win(c0), win(c0 + 1), win(c0 + 2), full((CONV_W, CONV_CH)), full((2, LANES)),
                  full((1, DN_DV))],
        out_specs=[pl.BlockSpec((GDN_CHUNK, DN_VW), lambda b, i: (b * nc + i, 0)),
                   pl.BlockSpec((None, DN_HEADS, DN_DK, DN_DV), lambda b, i: (b, 0, 0, 0))],
        out_shape=[jax.ShapeDtypeStruct((N_PROMPT, DN_VW), BF16),
                   jax.ShapeDtypeStruct((BATCH, DN_HEADS, DN_DK, DN_DV), F32)],
        scratch_shapes=[pltpu.VMEM((8 + GDN_CHUNK, CONV_CH), F32)],
        compiler_params=_params("arbitrary", "arbitrary"),
        name="gdn_prompt",
    )(proj, proj, proj, conv_w, gp, norm_w)


def _gdn_sample(proj, conv_w, gp, norm_w, conv_state, s0, chunk):
    row0 = N_PROMPT // DEC_SEQ
    c0 = COL_QKV // GDN_COLS

    def win(c):
        return pl.BlockSpec((DEC_SEQ, GDN_COLS), lambda b: (row0 + b, c))

    full = lambda shape: pl.BlockSpec(shape, lambda b: (0,) * len(shape))
    state = pl.BlockSpec((None, DN_HEADS, DN_DK, DN_DV), lambda b: (b, 0, 0, 0))
    st = jnp.pad(conv_state, ((0, 0), (8 - (CONV_W - 1), 0), (0, 0)))
    return pl.pallas_call(
        functools.partial(_gdn_kernel, chunk=chunk, rows_in=DEC_SEQ, has_state=True),
        grid=(DEC_BATCH,),
        in_specs=[win(c0), win(c0 + 1), win(c0 + 2), full((CONV_W, CONV_CH)), full((2, LANES)),
                  full((1, DN_DV)), pl.BlockSpec((None, 8, CONV_CH), lambda b: (b, 0, 0)), state],
        out_specs=[pl.BlockSpec((DEC_SEQ, DN_VW), lambda b: (b, 0)), state],
        out_shape=[jax.ShapeDtypeStruct((N_SAMPLE, DN_VW), F32),
                   jax.ShapeDtypeStruct((DEC_BATCH, DN_HEADS, DN_DK, DN_DV), F32)],
        scratch_shapes=[pltpu.VMEM((8 + chunk, CONV_CH), F32)],
        compiler_params=_params("arbitrary"),
        name="gdn_sample",
    )(proj, proj, proj, conv_w, gp, norm_w, st, s0)


def _out_router_kernel(ma_ref, md_ref, x_ref, wo_ref, g_ref, b_ref, wr_ref, br_ref,
                       x1_ref, eid_ref, gate_ref):
    mix = (jnp.dot(ma_ref[...], wo_ref[0:ATTN_WIDTH, :], preferred_element_type=F32)
           + jnp.dot(md_ref[...], wo_ref[ATTN_WIDTH:MIX_WIDTH, :], preferred_element_type=F32))
    x1 = _layer_norm(DEEPNORM_ALPHA * x_ref[...] + mix, g_ref[...], b_ref[...])
    x1b = x1.astype(BF16)
    x1_ref[...] = x1
    logits = jnp.dot(x1b, wr_ref[...], preferred_element_type=F32) + br_ref[...]
    lane = lax.broadcasted_iota(jnp.int32, logits.shape, 1)
    lane_f = lane.astype(F32)
    far = float(LANES)
    is_grp = lane < N_GROUPS
    grp_max = jnp.max(jnp.where(is_grp, logits, NEG_INF), axis=-1, keepdims=True)
    grp = jnp.min(jnp.where(is_grp & (logits == grp_max), lane_f, far), axis=-1, keepdims=True)
    grp_gate = 1.0 / jnp.sum(jnp.where(is_grp, jnp.exp(logits - grp_max), 0.0), axis=-1, keepdims=True)
    lo = N_GROUPS + EXPERTS_PER_GROUP * grp
    in_grp = (lane_f >= lo) & (lane_f < lo + EXPERTS_PER_GROUP)
    v1 = jnp.max(jnp.where(in_grp, logits, NEG_INF), axis=-1, keepdims=True)
    i1 = jnp.min(jnp.where(in_grp & (logits == v1), lane_f, far), axis=-1, keepdims=True)
    rest = in_grp & (lane_f != i1)
    v2 = jnp.max(jnp.where(rest, logits, NEG_INF), axis=-1, keepdims=True)
    i2 = jnp.min(jnp.where(rest & (logits == v2), lane_f, far), axis=-1, keepdims=True)
    t = jnp.exp(v2 - v1)
    g1 = grp_gate / (1.0 + t)
    g2 = g1 * t
    e1 = (i1 - N_GROUPS).astype(jnp.int32)
    e2 = (i2 - N_GROUPS).astype(jnp.int32)
    eid_ref[...] = jnp.where(lane == 0, e1, jnp.where(lane == 1, e2, 0))
    gate_ref[...] = jnp.where(lane == 0, g1, jnp.where(lane == 1, g2, 0.0))


def _out_router(mix_a, mix_d, x, wo_b, g, b, wr_b, br):
    def row(w):
        return pl.BlockSpec((ROW_TILE, w), lambda i: (i, 0))

    full = lambda shape: pl.BlockSpec(shape, lambda i: (0, 0))
    return pl.pallas_call(
        _out_router_kernel,
        grid=(N_TOK // ROW_TILE,),
        in_specs=[row(ATTN_WIDTH), row(DN_VW), row(D_MODEL), full((MIX_WIDTH, D_MODEL)),
                  full((1, D_MODEL)), full((1, D_MODEL)), full((D_MODEL, LANES)), full((1, LANES))],
        out_specs=[row(D_MODEL), row(LANES), row(LANES)],
        out_shape=[jax.ShapeDtypeStruct((N_TOK, D_MODEL), F32),
                   jax.ShapeDtypeStruct((N_TOK, LANES), jnp.int32),
                   jax.ShapeDtypeStruct((N_TOK, LANES), F32)],
        compiler_params=_params("arbitrary"),
        name="out_router",
    )(mix_a, mix_d, x, wo_b, g, b, wr_b, br)


def _plan_kernel(eid_ref, pos_ref, cnt_ref, run_ref, off_ref):
    phase = pl.program_id(0)
    i = pl.program_id(1)
    lane = lax.broadcasted_iota(jnp.int32, (ROW_TILE, LANES), 1)
    eid = eid_ref[...]
    hot0 = (lane == eid[:, 0:1]).astype(F32)
    hot1 = (lane == eid[:, 1:2]).astype(F32)
    hot = hot0 + hot1
    tile_cnt = jnp.sum(hot, axis=0, keepdims=True)

    @pl.when((phase == 0) & (i == 0))
    def _():
        cnt_ref[...] = jnp.zeros_like(cnt_ref)

    @pl.when(phase == 0)
    def _():
        cnt_ref[...] += tile_cnt

    @pl.when((phase == 1) & (i == 0))
    def _():
        cnt = cnt_ref[...]
        hi = jnp.floor(cnt * (1.0 / 256.0))
        lo = cnt - 256.0 * hi
        r = lax.broadcasted_iota(jnp.int32, (LANES, LANES), 0)
        c = lax.broadcasted_iota(jnp.int32, (LANES, LANES), 1)
        upper = (r < c).astype(F32)
        hi8 = jnp.broadcast_to(hi, (8, LANES))
        lo8 = jnp.broadcast_to(lo, (8, LANES))
        off = 256.0 * _mm(hi8, upper) + _mm(lo8, upper)
        off_ref[...] = off[0:1]
        run_ref[...] = jnp.zeros_like(run_ref)

    @pl.when(phase == 1)
    def _():
        r = lax.broadcasted_iota(jnp.int32, (ROW_TILE, ROW_TILE), 0)
        c = lax.broadcasted_iota(jnp.int32, (ROW_TILE, ROW_TILE), 1)
        before = (r > c).astype(F32)
        slot = _mm(before, hot) + run_ref[...] + off_ref[...]
        p0 = jnp.sum(hot0 * slot, axis=-1, keepdims=True).astype(jnp.int32)
        p1 = jnp.sum(hot1 * slot, axis=-1, keepdims=True).astype(jnp.int32)
        pos_ref[...] = jnp.where(lane == 0, p0, jnp.where(lane == 1, p1, 0))
        run_ref[...] += tile_cnt


def _plan(eid):
    return pl.pallas_call(
        _plan_kernel,
        grid=(2, N_TOK // ROW_TILE),
        in_specs=[pl.BlockSpec((ROW_TILE, LANES), lambda p, i: (i, 0))],
        out_specs=[pl.BlockSpec((ROW_TILE, LANES), lambda p, i: (i * p, 0)),
                   pl.BlockSpec((1, LANES), lambda p, i: (0, 0))],
        out_shape=[jax.ShapeDtypeStruct((N_TOK, LANES), jnp.int32),
                   jax.ShapeDtypeStruct((1, LANES), F32)],
        scratch_shapes=[pltpu.VMEM((1, LANES), F32), pltpu.VMEM((1, LANES), F32)],
        compiler_params=_params("arbitrary", "arbitrary"),
        name="moe_plan",
    )(eid)


def _work_items(counts):
    counts = counts.astype(jnp.int32)
    ends = jnp.cumsum(counts)
    starts = ends - counts
    first_tile = starts // MOE_TM
    n_tiles = jnp.where(counts > 0, (ends - 1) // MOE_TM - first_tile + 1, 0)
    item_end = jnp.cumsum(n_tiles)
    item_start = item_end - n_tiles
    n_items = item_end[-1]
    w = jnp.minimum(jnp.arange(N_MOE_ITEMS, dtype=jnp.int32), n_items - 1)
    expert = jnp.searchsorted(item_end, w, side="right").astype(jnp.int32)
    tile = first_tile[expert] + (w - item_start[expert])
    prev_tile = jnp.concatenate([jnp.full((1,), -1, jnp.int32), tile[:-1]])
    prev_expert = jnp.concatenate([jnp.full((1,), -1, jnp.int32), expert[:-1]])
    valid = (jnp.arange(N_MOE_ITEMS, dtype=jnp.int32) < n_items).astype(jnp.int32)
    return (tile, expert, (tile != prev_tile).astype(jnp.int32), (expert != prev_expert).astype(jnp.int32),
            valid, starts[expert], ends[expert])


def _dispatch_kernel(pos_ref, x_ref, xs_ref, sem):
    base = pl.program_id(0) * (2 * ROW_TILE)

    def row_copy(r, slot):
        return pltpu.make_async_copy(x_ref.at[pl.ds(r, 1), :], xs_ref.at[pl.ds(slot, 1), :], sem)

    def issue(r, carry):
        row_copy(r, pos_ref[base + 2 * r]).start()
        row_copy(r, pos_ref[base + 2 * r + 1]).start()
        return carry

    def drain(r, carry):
        row_copy(0, 0).wait()
        row_copy(0, 0).wait()
        return carry

    lax.fori_loop(0, ROW_TILE, issue, 0)
    lax.fori_loop(0, ROW_TILE, drain, 0)


def _dispatch(pos_flat, x1):
    return pl.pallas_call(
        _dispatch_kernel,
        grid_spec=pltpu.PrefetchScalarGridSpec(
            num_scalar_prefetch=1,
            grid=(N_TOK // ROW_TILE,),
            in_specs=[pl.BlockSpec((ROW_TILE, D_MODEL), lambda i, pos: (i, 0))],
            out_specs=pl.BlockSpec(memory_space=pl.ANY),
            scratch_shapes=[pltpu.SemaphoreType.DMA(())]),
        out_shape=jax.ShapeDtypeStruct((N_ASSIGN, D_MODEL), F32),
        compiler_params=_params("arbitrary"),
        name="moe_dispatch",
    )(pos_flat, x1)


def _experts_kernel(tile_ref, exp_ref, first_ref, newexp_ref, valid_ref, gstart_ref, gend_ref,
                    xs_ref, wgu_ref, wdn_ref, o_ref, wgu_b, wdn_b):
    w = pl.program_id(0)

    @pl.when(valid_ref[w] == 1)
    def _():
        @pl.when(newexp_ref[w] == 1)
        def _():
            wgu_b[...] = wgu_ref[...].astype(BF16)
            wdn_b[...] = wdn_ref[...].astype(BF16)

        h = jnp.dot(xs_ref[...].astype(BF16), wgu_b[...], preferred_element_type=F32)
        act = _silu(h[:, :D_FF_EXPERT]) * h[:, D_FF_EXPERT:]
        y = jnp.dot(act.astype(BF16), wdn_b[...], preferred_element_type=F32)
        rows = tile_ref[w] * MOE_TM + lax.broadcasted_iota(jnp.int32, (MOE_TM, 1), 0)
        mine = (rows >= gstart_ref[w]) & (rows < gend_ref[w])
        y = jnp.where(mine, y, 0.0)

        @pl.when(first_ref[w] == 1)
        def _():
            o_ref[...] = y

        @pl.when(first_ref[w] == 0)
        def _():
            o_ref[...] += y


def _experts(items, xs, w_gu, w_dn):
    return pl.pallas_call(
        _experts_kernel,
        grid_spec=pltpu.PrefetchScalarGridSpec(
            num_scalar_prefetch=7,
            grid=(N_MOE_ITEMS,),
            in_specs=[pl.BlockSpec((MOE_TM, D_MODEL), lambda w, t, e, *_: (t[w], 0)),
                      pl.BlockSpec((None, D_MODEL, 2 * D_FF_EXPERT), lambda w, t, e, *_: (e[w], 0, 0)),
                      pl.BlockSpec((None, D_FF_EXPERT, D_MODEL), lambda w, t, e, *_: (e[w], 0, 0))],
            out_specs=pl.BlockSpec((MOE_TM, D_MODEL), lambda w, t, e, *_: (t[w], 0)),
            scratch_shapes=[pltpu.VMEM((D_MODEL, 2 * D_FF_EXPERT), BF16),
                            pltpu.VMEM((D_FF_EXPERT, D_MODEL), BF16)]),
        out_shape=jax.ShapeDtypeStruct((N_ASSIGN, D_MODEL), F32),
        compiler_params=_params("arbitrary"),
        name="moe_experts",
    )(*items, xs, w_gu, w_dn)


def _combine_kernel(pos_ref, ys_ref, x1_ref, gate_ref, g_ref, b_ref, x2_ref, x2b_ref, r0_ref, r1_ref, sem):
    base = pl.program_id(0) * (2 * ROW_TILE)

    def row_copy(slot, dst, r):
        return pltpu.make_async_copy(ys_ref.at[pl.ds(slot, 1), :], dst.at[pl.ds(r, 1), :], sem)

    def issue(r, carry):
        row_copy(pos_ref[base + 2 * r], r0_ref, r).start()
        row_copy(pos_ref[base + 2 * r + 1], r1_ref, r).start()
        return carry

    def drain(r, carry):
        row_copy(0, r0_ref, 0).wait()
        row_copy(0, r1_ref, 0).wait()
        return carry

    lax.fori_loop(0, ROW_TILE, issue, 0)
    lax.fori_loop(0, ROW_TILE, drain, 0)
    gate = gate_ref[...]
    y = gate[:, 0:1] * r0_ref[...] + gate[:, 1:2] * r1_ref[...]
    x2 = _layer_norm(DEEPNORM_ALPHA * x1_ref[...] + y, g_ref[...], b_ref[...])
    x2_ref[...] = x2
    x2b_ref[...] = x2.astype(BF16)


def _combine(pos_flat, ys, x1, gates, g, b):
    row = lambda w: pl.BlockSpec((ROW_TILE, w), lambda i, pos: (i, 0))
    vec = pl.BlockSpec((1, D_MODEL), lambda i, pos: (0, 0))
    return pl.pallas_call(
        _combine_kernel,
        grid_spec=pltpu.PrefetchScalarGridSpec(
            num_scalar_prefetch=1,
            grid=(N_TOK // ROW_TILE,),
            in_specs=[pl.BlockSpec(memory_space=pl.ANY), row(D_MODEL), row(LANES), vec, vec],
            out_specs=[row(D_MODEL), row(D_MODEL)],
            scratch_shapes=[pltpu.VMEM((ROW_TILE, D_MODEL), F32), pltpu.VMEM((ROW_TILE, D_MODEL), F32),
                            pltpu.SemaphoreType.DMA(())]),
        out_shape=[jax.ShapeDtypeStruct((N_TOK, D_MODEL), F32),
                   jax.ShapeDtypeStruct((N_TOK, D_MODEL), BF16)],
        compiler_params=_params("arbitrary"),
        name="moe_combine",
    )(pos_flat, ys, x1, gates, g, b)


def _router_weights(wg, bg, we, be):
    wr = jnp.zeros((D_MODEL, LANES), F32)
    wr = wr.at[:, :N_GROUPS].set(wg).at[:, N_GROUPS:N_GROUPS + N_EXPERTS].set(we)
    br = jnp.zeros((1, LANES), F32)
    br = br.at[0, :N_GROUPS].set(bg).at[0, N_GROUPS:N_GROUPS + N_EXPERTS].set(be)
    return wr.astype(BF16), br


def kernel(x_prompt, x_sample, cache_swa_k, cache_swa_v, state_conv, state_delta, ln_in_g, ln_in_b, w_in, conv_w,
           a_log, dt_bias, dn_norm_w, attn_sinks, w_out, ln1_g, ln1_b, router_group_w, router_group_b,
           router_expert_w, router_expert_b, w_gate_up, w_down, ln2_g, ln2_b):
    x_all = jnp.concatenate([x_prompt.reshape(N_PROMPT, D_MODEL), x_sample.reshape(N_SAMPLE, D_MODEL)], axis=0)
    x, xb = _ln_in(x_all, ln_in_g, ln_in_b)
    kp, vp, cp, sp = [], [], [], []
    kq, vq, cq, sq = [], [], [], []
    for l in range(DEPTH):
        proj = _proj(xb, w_in[l])
        sinks = attn_sinks[l].astype(F32)
        o_attn_p = _attn_prompt(proj, sinks)
        o_attn_s, k_s, v_s = _attn_sample(proj, sinks,
                                          cache_swa_k[l].reshape(DEC_BATCH, WINDOW, KV_WIDTH),
                                          cache_swa_v[l].reshape(DEC_BATCH, WINDOW, KV_WIDTH))
        gp = _gdn_gate_params(a_log[l], dt_bias[l])
        norm_w = dn_norm_w[l].reshape(1, DN_DV).astype(F32)
        o_dn_p, s_p = _gdn_prompt(proj, conv_w[l], gp, norm_w)
        o_dn_s, s_s = _gdn_sample(proj, conv_w[l], gp, norm_w, state_conv[l], state_delta[l], GDN_CHUNK)
        mix_a = jnp.concatenate([o_attn_p, o_attn_s.astype(BF16)], axis=0)
        mix_d = jnp.concatenate([o_dn_p, o_dn_s.astype(BF16)], axis=0)
        wr_b, br = _router_weights(router_group_w[l], router_group_b[l], router_expert_w[l], router_expert_b[l])
        x1, eid, gates = _out_router(mix_a, mix_d, x, w_out[l].astype(BF16),
                                          ln1_g[l].reshape(1, D_MODEL), ln1_b[l].reshape(1, D_MODEL), wr_b, br)
        pos, counts = _plan(eid)
        pos_flat = pos[:, :2].reshape(N_ASSIGN)
        items = _work_items(counts[0, :N_EXPERTS])
        xs = _dispatch(pos_flat, x1)
        ys = _experts(items, xs, w_gate_up[l], w_down[l])
        x, xb = _combine(pos_flat, ys, x1, gates, ln2_g[l].reshape(1, D_MODEL), ln2_b[l].reshape(1, D_MODEL))

        pp = proj[:N_PROMPT].reshape(BATCH, SEQ, IN_COLS)
        ps = proj[N_PROMPT:].reshape(DEC_BATCH, DEC_SEQ, IN_COLS)
        kp.append(pp[:, SEQ - WINDOW:, COL_K:COL_V].reshape(BATCH, WINDOW, KV_HEADS, HEAD_DIM))
        vp.append(pp[:, SEQ - WINDOW:, COL_V:COL_QKV].reshape(BATCH, WINDOW, KV_HEADS, HEAD_DIM))
        cp.append(pp[:, SEQ - (CONV_W - 1):, COL_QKV:COL_Z])
        sp.append(s_p)
        kq.append(k_s.reshape(DEC_BATCH, WINDOW, KV_HEADS, HEAD_DIM))
        vq.append(v_s.reshape(DEC_BATCH, WINDOW, KV_HEADS, HEAD_DIM))
        cq.append(ps[:, DEC_SEQ - (CONV_W - 1):, COL_QKV:COL_Z])
        sq.append(s_s)
    return (x[:N_PROMPT].reshape(BATCH, SEQ, D_MODEL), x[N_PROMPT:].reshape(DEC_BATCH, DEC_SEQ, D_MODEL),
            jnp.stack(kp), jnp.stack(vp), jnp.stack(cp), jnp.stack(sp),
            jnp.stack(kq), jnp.stack(vq), jnp.stack(cq), jnp.stack(sq))
```

```python
import functools

import jax
import jax.numpy as jnp
from jax import lax
from jax.experimental import pallas as pl
from jax.experimental.pallas import tpu as pltpu

D_MODEL = 2048
BATCH = 4
SEQ = 2048
DEPTH = 4
DEC_BATCH = 32
DEC_SEQ = 8
HEAD_DIM = 128
ATTN_HEADS = 8
KV_HEADS = 2
GQA = ATTN_HEADS // KV_HEADS
ATTN_WIDTH = ATTN_HEADS * HEAD_DIM
KV_WIDTH = KV_HEADS * HEAD_DIM
WINDOW = 128
DN_HEADS = 8
DN_DK = 128
DN_DV = 128
DN_KW = DN_HEADS * DN_DK
DN_VW = DN_HEADS * DN_DV
CONV_W = 4
CONV_CH = 2 * DN_KW + DN_VW
MIX_WIDTH = ATTN_WIDTH + DN_VW
IN_COLS = ATTN_WIDTH + 2 * KV_WIDTH + CONV_CH + DN_VW + 2 * DN_HEADS
N_GROUPS = 4
EXPERTS_PER_GROUP = 8
N_EXPERTS = N_GROUPS * EXPERTS_PER_GROUP
D_FF_EXPERT = 256
DEEPNORM_ALPHA = (2 * DEPTH) ** 0.25
LN_EPS = 1e-5
RMS_EPS = 1e-6
NEG_INF = -1e30

N_PROMPT = BATCH * SEQ
N_SAMPLE = DEC_BATCH * DEC_SEQ
N_TOK = N_PROMPT + N_SAMPLE
N_ASSIGN = 2 * N_TOK

COL_K = ATTN_WIDTH
COL_V = COL_K + KV_WIDTH
COL_QKV = COL_V + KV_WIDTH
COL_Z = COL_QKV + CONV_CH
COL_AB = COL_Z + DN_VW

LANES = 128
ROW_TILE = 256
PROJ_TM = 768
PROJ_TN = 1024
OUT_TM = 768
OUT_SUB = 256
GDN_CHUNK = 128
GDN_CHUNK_SAMPLE = 16
GDN_INV_BASE = 16
GDN_COLS = 1536
SAMPLE_BB = 4
MOE_TM = 256
N_MOE_TILES = N_ASSIGN // MOE_TM
N_MOE_ITEMS = N_MOE_TILES + N_EXPERTS - 1
VMEM_LIMIT = 48 * 1024 * 1024
VMEM_LIMIT_BIG = 56 * 1024 * 1024

F32 = jnp.float32
BF16 = jnp.bfloat16


def _params(*sem, vmem_limit=VMEM_LIMIT):
    return pltpu.CompilerParams(dimension_semantics=sem, vmem_limit_bytes=vmem_limit)


def _mm(a, b):
    return jnp.dot(a.astype(BF16), b.astype(BF16), preferred_element_type=F32)


def _mm_nt(a, b):
    return lax.dot_general(a.astype(BF16), b.astype(BF16), (((1,), (1,)), ((), ())),
                           preferred_element_type=F32)


def _mm_tn(a, b):
    return lax.dot_general(a.astype(BF16), b.astype(BF16), (((0,), (0,)), ((), ())),
                           preferred_element_type=F32)


def _layer_slab(shape, layer):
    return pl.BlockSpec((None,) + tuple(shape), lambda *_: (layer,) + (0,) * len(shape),
                        pipeline_mode=pl.Buffered(1))


def _layer_norm(x, g, b):
    mu = jnp.mean(x, axis=-1, keepdims=True)
    xc = x - mu
    var = jnp.mean(xc * xc, axis=-1, keepdims=True)
    return xc * lax.rsqrt(var + LN_EPS) * g + b


def _silu(x):
    return x / (1.0 + jnp.exp(-x))


def _ln_in_kernel(xp_ref, xs_ref, g_ref, b_ref, o_ref, ob_ref):
    def emit(x_ref):
        y = _layer_norm(x_ref[...], g_ref[...], b_ref[...])
        o_ref[...] = y
        ob_ref[...] = y.astype(BF16)

    is_prompt = pl.program_id(0) < N_PROMPT // ROW_TILE
    pl.when(is_prompt)(lambda: emit(xp_ref))
    pl.when(jnp.logical_not(is_prompt))(lambda: emit(xs_ref))


def _ln_in(xp, xs, g, b):
    assert N_SAMPLE == ROW_TILE
    row = pl.BlockSpec((ROW_TILE, D_MODEL), lambda i: (i, 0))
    vec = pl.BlockSpec((1, D_MODEL), lambda i: (0, 0))
    last_prompt = N_PROMPT // ROW_TILE - 1
    return pl.pallas_call(
        _ln_in_kernel,
        grid=(N_TOK // ROW_TILE,),
        in_specs=[pl.BlockSpec((ROW_TILE, D_MODEL), lambda i: (jnp.minimum(i, last_prompt), 0)),
                  pl.BlockSpec((ROW_TILE, D_MODEL), lambda i: (0, 0)), vec, vec],
        out_specs=[row, row],
        out_shape=[jax.ShapeDtypeStruct((N_TOK, D_MODEL), F32),
                   jax.ShapeDtypeStruct((N_TOK, D_MODEL), BF16)],
        compiler_params=_params("arbitrary"),
        name="ln_in",
    )(xp, xs, g.reshape(1, D_MODEL), b.reshape(1, D_MODEL))


def _proj_kernel(x_ref, w_ref, o_ref, wb_ref):
    @pl.when(pl.program_id(1) == 0)
    def _():
        wb_ref[...] = w_ref[...].astype(BF16)

    o_ref[...] = jnp.dot(x_ref[...], wb_ref[...], preferred_element_type=F32)


def _proj(xb, w_in, layer):
    return pl.pallas_call(
        _proj_kernel,
        grid=(pl.cdiv(IN_COLS, PROJ_TN), N_TOK // PROJ_TM),
        in_specs=[pl.BlockSpec((PROJ_TM, D_MODEL), lambda j, i: (i, 0)),
                  pl.BlockSpec((None, D_MODEL, PROJ_TN), lambda j, i: (layer, 0, j))],
        out_specs=pl.BlockSpec((PROJ_TM, PROJ_TN), lambda j, i: (i, j)),
        out_shape=jax.ShapeDtypeStruct((N_TOK, IN_COLS), F32),
        scratch_shapes=[pltpu.VMEM((D_MODEL, PROJ_TN), BF16)],
        compiler_params=_params("arbitrary", "arbitrary"),
        name="in_proj",
    )(xb, w_in)


def _head_column(values, rows_per_head, n_rows):
    grp = lax.broadcasted_iota(jnp.int32, (n_rows, 1), 0) // rows_per_head
    col = jnp.full((n_rows, 1), values[-1], F32)
    for g in range(len(values) - 2, -1, -1):
        col = jnp.where(grp == g, values[g], col)
    return col


def _softmax_with_sink(s, sink_col):
    m = jnp.maximum(jnp.max(s, axis=-1, keepdims=True), sink_col)
    p = jnp.exp(s - m)
    denom = jnp.sum(p, axis=-1, keepdims=True) + jnp.exp(sink_col - m)
    return p, denom


def _attn_prompt_kernel(sink_ref, q_ref, kc_ref, kp_ref, vc_ref, vp_ref, o_ref):
    blk = pl.program_id(1)
    n_rows = GQA * WINDOW
    qpos = lax.broadcasted_iota(jnp.int32, (n_rows, 2 * WINDOW), 0) % WINDOW
    kidx = lax.broadcasted_iota(jnp.int32, (n_rows, 2 * WINDOW), 1)
    dist_i = qpos + WINDOW - kidx
    valid = (dist_i >= 0) & (dist_i <= WINDOW) & ((kidx >= WINDOW) | (blk > 0))
    dist = dist_i.astype(F32)
    for kvh in range(KV_HEADS):
        cols = slice(kvh * HEAD_DIM, (kvh + 1) * HEAD_DIM)
        kk = jnp.concatenate([kp_ref[:, cols], kc_ref[:, cols]], axis=0)
        vv = jnp.concatenate([vp_ref[:, cols], vc_ref[:, cols]], axis=0)
        heads = [kvh * GQA + g for g in range(GQA)]
        qs = jnp.concatenate([q_ref[:, h * HEAD_DIM:(h + 1) * HEAD_DIM] for h in heads], axis=0)
        slope = _head_column([2.0 ** (-8.0 * (h + 1) / ATTN_HEADS) for h in heads], WINDOW, n_rows)
        sink = _head_column([sink_ref[h] for h in heads], WINDOW, n_rows)
        s = _mm_nt(qs, kk) * (HEAD_DIM ** -0.5) - slope * dist
        s = jnp.where(valid, s, NEG_INF)
        p, denom = _softmax_with_sink(s, sink)
        o = _mm(p, vv) / denom
        for g, h in enumerate(heads):
            o_ref[:, h * HEAD_DIM:(h + 1) * HEAD_DIM] = o[g * WINDOW:(g + 1) * WINDOW].astype(BF16)


def _attn_prompt(proj, sinks):
    nb = SEQ // WINDOW
    kcol, vcol = COL_K // KV_WIDTH, COL_V // KV_WIDTH

    def cur(c):
        return lambda b, i: (b * nb + i, c)

    def prev(c):
        return lambda b, i: (jnp.maximum(b * nb + i - 1, 0), c)

    return pl.pallas_call(
        _attn_prompt_kernel,
        grid=(BATCH, nb),
        in_specs=[pl.BlockSpec(memory_space=pltpu.SMEM),
                  pl.BlockSpec((WINDOW, ATTN_WIDTH), cur(0)),
                  pl.BlockSpec((WINDOW, KV_WIDTH), cur(kcol)),
                  pl.BlockSpec((WINDOW, KV_WIDTH), prev(kcol)),
                  pl.BlockSpec((WINDOW, KV_WIDTH), cur(vcol)),
                  pl.BlockSpec((WINDOW, KV_WIDTH), prev(vcol))],
        out_specs=pl.BlockSpec((WINDOW, ATTN_WIDTH), lambda b, i: (b * nb + i, 0)),
        out_shape=jax.ShapeDtypeStruct((N_PROMPT, ATTN_WIDTH), BF16),
        compiler_params=_params("arbitrary", "arbitrary"),
        name="attn_prompt",
    )(sinks, proj, proj, proj, proj, proj)


def _attn_sample_kernel(sink_ref, q_ref, kn_ref, vn_ref, kb_ref, vb_ref, o_ref, ko_ref, vo_ref):
    n_rows = GQA * DEC_SEQ
    tpos = lax.broadcasted_iota(jnp.int32, (n_rows, 2 * WINDOW), 0) % DEC_SEQ
    kidx = lax.broadcasted_iota(jnp.int32, (n_rows, 2 * WINDOW), 1)
    dist_i = jnp.where(kidx < WINDOW, tpos + WINDOW - kidx, tpos - (kidx - WINDOW))
    valid = (dist_i >= 0) & (dist_i <= WINDOW) & (kidx < WINDOW + DEC_SEQ)
    dist = dist_i.astype(F32)
    pad = jnp.zeros((WINDOW - DEC_SEQ, HEAD_DIM), F32)
    for bb in range(SAMPLE_BB):
        rows = slice(bb * DEC_SEQ, (bb + 1) * DEC_SEQ)
        for kvh in range(KV_HEADS):
            cols = slice(kvh * HEAD_DIM, (kvh + 1) * HEAD_DIM)
            kk = jnp.concatenate([kb_ref[bb, :, cols], kn_ref[rows, cols], pad], axis=0)
            vv = jnp.concatenate([vb_ref[bb, :, cols], vn_ref[rows, cols], pad], axis=0)
            heads = [kvh * GQA + g for g in range(GQA)]
            qs = jnp.concatenate([q_ref[rows, h * HEAD_DIM:(h + 1) * HEAD_DIM] for h in heads], axis=0)
            slope = _head_column([2.0 ** (-8.0 * (h + 1) / ATTN_HEADS) for h in heads], DEC_SEQ, n_rows)
            sink = _head_column([sink_ref[h] for h in heads], DEC_SEQ, n_rows)
            s = _mm_nt(qs, kk) * (HEAD_DIM ** -0.5) - slope * dist
            s = jnp.where(valid, s, NEG_INF)
            p, denom = _softmax_with_sink(s, sink)
            o = _mm(p, vv) / denom
            for g, h in enumerate(heads):
                o_ref[rows, h * HEAD_DIM:(h + 1) * HEAD_DIM] = o[g * DEC_SEQ:(g + 1) * DEC_SEQ]
        ko_ref[bb, 0:WINDOW - DEC_SEQ, :] = kb_ref[bb, DEC_SEQ:WINDOW, :]
        ko_ref[bb, WINDOW - DEC_SEQ:WINDOW, :] = kn_ref[rows, :]
        vo_ref[bb, 0:WINDOW - DEC_SEQ, :] = vb_ref[bb, DEC_SEQ:WINDOW, :]
        vo_ref[bb, WINDOW - DEC_SEQ:WINDOW, :] = vn_ref[rows, :]


def _attn_sample(proj, sinks, k_buf, v_buf, layer):
    rows = SAMPLE_BB * DEC_SEQ
    row0 = N_PROMPT // rows
    kcol, vcol = COL_K // KV_WIDTH, COL_V // KV_WIDTH
    cache_in = pl.BlockSpec((None, SAMPLE_BB, WINDOW, KV_WIDTH), lambda i: (layer, i, 0, 0))
    cache = pl.BlockSpec((SAMPLE_BB, WINDOW, KV_WIDTH), lambda i: (i, 0, 0))
    cache_shape = jax.ShapeDtypeStruct((DEC_BATCH, WINDOW, KV_WIDTH), F32)
    return pl.pallas_call(
        _attn_sample_kernel,
        grid=(DEC_BATCH // SAMPLE_BB,),
        in_specs=[pl.BlockSpec(memory_space=pltpu.SMEM),
                  pl.BlockSpec((rows, ATTN_WIDTH), lambda i: (row0 + i, 0)),
                  pl.BlockSpec((rows, KV_WIDTH), lambda i: (row0 + i, kcol)),
                  pl.BlockSpec((rows, KV_WIDTH), lambda i: (row0 + i, vcol)),
                  cache_in, cache_in],
        out_specs=[pl.BlockSpec((rows, ATTN_WIDTH), lambda i: (i, 0)), cache, cache],
        out_shape=[jax.ShapeDtypeStruct((N_SAMPLE, ATTN_WIDTH), F32), cache_shape, cache_shape],
        compiler_params=_params("arbitrary"),
        name="attn_sample",
    )(sinks, proj, proj, proj, k_buf, v_buf)


def _cumsum_rows(x):
    n = x.shape[0]
    row = lax.broadcasted_iota(jnp.int32, x.shape, 0)
    step = 1
    while step < n:
        x = x + jnp.where(row >= step, pltpu.roll(x, step, axis=0), 0.0)
        step *= 2
    return x


def _gdn_kernel(*refs, chunk, rows_in, has_state):
    if has_state:
        b1_ref, b2_ref, b3_ref, cw_ref, gp_ref, nw_ref, st_ref, s0_ref, o_ref, s_ref, ext_ref = refs
        first = True
    else:
        b1_ref, b2_ref, b3_ref, cw_ref, gp_ref, nw_ref, o_ref, s_ref, ext_ref = refs
        first = pl.program_id(1) == 0

    if has_state:
        ext_ref[0:8, :] = st_ref[...]
        s_ref[...] = s0_ref[...]
    else:
        @pl.when(first)
        def _():
            ext_ref[0:8, :] = jnp.zeros((8, CONV_CH), F32)
            s_ref[...] = jnp.zeros_like(s_ref)

        @pl.when(jnp.logical_not(first))
        def _():
            ext_ref[0:8, :] = ext_ref[chunk:chunk + 8, :]

    ext_ref[8:8 + rows_in, 0:GDN_COLS] = b1_ref[...]
    ext_ref[8:8 + rows_in, GDN_COLS:CONV_CH] = b2_ref[...]
    if rows_in < chunk:
        ext_ref[8 + rows_in:8 + chunk, :] = jnp.zeros((chunk - rows_in, CONV_CH), F32)

    lane = lax.broadcasted_iota(jnp.int32, (chunk, LANES), 1)
    row = lax.broadcasted_iota(jnp.int32, (chunk, LANES), 0)
    if rows_in < chunk:
        ab = jnp.concatenate([b3_ref[:, DN_VW:DN_VW + LANES],
                              jnp.zeros((chunk - rows_in, LANES), F32)], axis=0)
    else:
        ab = b3_ref[:, DN_VW:DN_VW + LANES]
    live = (lane < 2 * DN_HEADS) & (row < rows_in)
    ab = jnp.where(live, ab, 0.0)
    a_scale = gp_ref[0:1, :]
    dt_bias = gp_ref[1:2, :]
    g_all = jnp.where(live, a_scale * jax.nn.softplus(ab + dt_bias), 0.0)
    beta_all = jnp.where(live, jax.nn.sigmoid(ab), 0.0)
    gcum = _cumsum_rows(g_all)
    gcum_t = gcum.T
    g_last_all = gcum[chunk - 1:chunk, :]

    ri = lax.broadcasted_iota(jnp.int32, (chunk, chunk), 0)
    ci = lax.broadcasted_iota(jnp.int32, (chunk, chunk), 1)
    incl = ri >= ci
    strict = ri > ci

    def conv_strip(col0):
        cols = slice(col0, col0 + LANES)
        acc = ext_ref[8:8 + chunk, cols] * cw_ref[CONV_W - 1:CONV_W, cols]
        for tap in range(1, CONV_W):
            acc = acc + ext_ref[8 - tap:8 - tap + chunk, cols] * cw_ref[CONV_W - 1 - tap:CONV_W - tap, cols]
        return _silu(acc)

    def l2norm(x):
        return x * lax.rsqrt(jnp.sum(x * x, axis=-1, keepdims=True) + RMS_EPS)

    heads = range(DN_HEADS)
    q = [l2norm(conv_strip(h * DN_DK)) * (DN_DK ** -0.5) for h in heads]
    k = [l2norm(conv_strip(DN_KW + h * DN_DK)) for h in heads]
    v = [conv_strip(2 * DN_KW + h * DN_DV) for h in heads]
    gc_col = [gcum[:, h:h + 1] for h in heads]
    g_last = [g_last_all[:, h:h + 1] for h in heads]
    beta = [beta_all[:, DN_HEADS + h:DN_HEADS + h + 1] for h in heads]
    decay = [jnp.where(incl, jnp.exp(jnp.where(incl, gc_col[h] - gcum_t[h:h + 1, :], 0.0)), 0.0) for h in heads]
    eg = [jnp.exp(gc_col[h]) for h in heads]
    kb = [k[h] * beta[h] for h in heads]
    kk = [_mm_nt(jnp.concatenate([kb[h], q[h]], axis=0), k[h]) for h in heads]
    qk = [jnp.where(incl, kk[h][chunk:] * decay[h], 0.0) for h in heads]
    neg_a = [jnp.where(strict, -kk[h][:chunk] * decay[h], 0.0) for h in heads]
    base = min(GDN_INV_BASE, chunk)
    p = [jnp.where(ri // base == ci // base, neg_a[h], 0.0) for h in heads]
    eye = (ri == ci).astype(F32)
    inv = [eye + p[h] for h in heads]
    span = 2
    while span < base:
        p = [_mm(p[h], p[h]) for h in heads]
        inv = [inv[h] + _mm(inv[h], p[h]) for h in heads]
        span *= 2
    size = base
    while size < chunk:
        lower_left = (ri // (2 * size) == ci // (2 * size)) & (ri // size != ci // size)
        off = [jnp.where(lower_left, neg_a[h], 0.0) for h in heads]
        inv = [inv[h] + _mm(_mm(inv[h], off[h]), inv[h]) for h in heads]
        size *= 2
    sol = [_mm(inv[h], jnp.concatenate([v[h] * beta[h], kb[h] * eg[h]], axis=1)) for h in heads]
    s_prev = [s_ref[h] for h in heads]
    ws = [_mm(jnp.concatenate([sol[h][:, DN_DV:], q[h] * eg[h]], axis=0), s_prev[h]) for h in heads]
    v_new = [sol[h][:, :DN_DV] - ws[h][:chunk] for h in heads]
    o = [ws[h][chunk:] + _mm(qk[h], v_new[h]) for h in heads]
    k_dec = [k[h] * jnp.exp(g_last[h] - gc_col[h]) for h in heads]
    for h in heads:
        s_ref[h] = s_prev[h] * jnp.exp(g_last[h]) + _mm_tn(k_dec[h], v_new[h])
    for h in heads:
        on = o[h] * lax.rsqrt(jnp.mean(o[h] * o[h], axis=-1, keepdims=True) + RMS_EPS)
        z = b3_ref[:, h * DN_DV:(h + 1) * DN_DV]
        out = on[0:rows_in] * nw_ref[...] * _silu(z)
        o_ref[:, h * DN_DV:(h + 1) * DN_DV] = out.astype(o_ref.dtype)


def _gdn_gate_params(a_log, dt_bias):
    rows = jnp.stack([-jnp.exp(a_log.astype(F32)), dt_bias.astype(F32)], axis=1)
    return jnp.pad(rows, ((0, 0), (0, 0), (0, LANES - DN_HEADS)))


def _gdn_prompt(proj, conv_w, gp, norm_w, layer):
    nc = SEQ // GDN_CHUNK
    c0 = COL_QKV // GDN_COLS

    def win(c):
        return pl.BlockSpec((GDN_CHUNK, GDN_COLS), lambda b, i: (b * nc + i, c))

    return pl.pallas_call(
        functools.partial(_gdn_kernel, chunk=GDN_CHUNK, rows_in=GDN_CHUNK, has_state=False),
        grid=(BATCH, nc),
        in_specs=[win(c0), win(c0 + 1), win(c0 + 2), _layer_slab((CONV_W, CONV_CH), layer),
                  _layer_slab((2, LANES), layer), _layer_slab((1, DN_DV), layer)],
        out_specs=[pl.BlockSpec((GDN_CHUNK, DN_VW), lambda b, i: (b * nc + i, 0)),
                   pl.BlockSpec((None, DN_HEADS, DN_DK, DN_DV), lambda b, i: (b, 0, 0, 0))],
        out_shape=[jax.ShapeDtypeStruct((N_PROMPT, DN_VW), BF16),
                   jax.ShapeDtypeStruct((BATCH, DN_HEADS, DN_DK, DN_DV), F32)],
        scratch_shapes=[pltpu.VMEM((8 + GDN_CHUNK, CONV_CH), F32)],
        compiler_params=_params("arbitrary", "arbitrary"),
        name="gdn_prompt",
    )(proj, proj, proj, conv_w, gp, norm_w)


def _gdn_sample(proj, conv_w, gp, norm_w, conv_state8, s0, chunk, layer):
    row0 = N_PROMPT // DEC_SEQ
    c0 = COL_QKV // GDN_COLS

    def win(c):
        return pl.BlockSpec((DEC_SEQ, GDN_COLS), lambda b: (row0 + b, c))

    state = pl.BlockSpec((None, DN_HEADS, DN_DK, DN_DV), lambda b: (b, 0, 0, 0))
    return pl.pallas_call(
        functools.partial(_gdn_kernel, chunk=chunk, rows_in=DEC_SEQ, has_state=True),
        grid=(DEC_BATCH,),
        in_specs=[win(c0), win(c0 + 1), win(c0 + 2), _layer_slab((CONV_W, CONV_CH), layer),
                  _layer_slab((2, LANES), layer), _layer_slab((1, DN_DV), layer),
                  pl.BlockSpec((None, None, 8, CONV_CH), lambda b: (layer, b, 0, 0)),
                  pl.BlockSpec((None, None, DN_HEADS, DN_DK, DN_DV), lambda b: (layer, b, 0, 0, 0))],
        out_specs=[pl.BlockSpec((DEC_SEQ, DN_VW), lambda b: (b, 0)), state],
        out_shape=[jax.ShapeDtypeStruct((N_SAMPLE, DN_VW), F32),
                   jax.ShapeDtypeStruct((DEC_BATCH, DN_HEADS, DN_DK, DN_DV), F32)],
        scratch_shapes=[pltpu.VMEM((8 + chunk, CONV_CH), F32)],
        compiler_params=_params("arbitrary"),
        name="gdn_sample",
    )(proj, proj, proj, conv_w, gp, norm_w, conv_state8, s0)


def _out_router_kernel(ma_ref, md_ref, x_ref, wo_ref, g_ref, b_ref, wr_ref, br_ref,
                       x1_ref, eid_ref, gate_ref):
    n_sub = OUT_TM // OUT_SUB

    def mix_of(c):
        rows = slice(c * OUT_SUB, (c + 1) * OUT_SUB)
        return (jnp.dot(ma_ref[rows, :], wo_ref[0:ATTN_WIDTH, :], preferred_element_type=F32)
                + jnp.dot(md_ref[rows, :], wo_ref[ATTN_WIDTH:MIX_WIDTH, :], preferred_element_type=F32))

    mix = mix_of(0)
    for c in range(n_sub):
        nxt = mix_of(c + 1) if c + 1 < n_sub else None
        rows = slice(c * OUT_SUB, (c + 1) * OUT_SUB)
        x1 = _layer_norm(DEEPNORM_ALPHA * x_ref[rows, :] + mix, g_ref[...], b_ref[...])
        x1_ref[rows, :] = x1
        logits = jnp.dot(x1.astype(BF16), wr_ref[...], preferred_element_type=F32) + br_ref[...]
        eid, gate = _route(logits)
        eid_ref[rows, :] = eid
        gate_ref[rows, :] = gate
        mix = nxt


def _route(logits):
    lane = lax.broadcasted_iota(jnp.int32, logits.shape, 1)
    lane_f = lane.astype(F32)
    far = float(LANES)
    is_grp = lane < N_GROUPS
    grp_max = jnp.max(jnp.where(is_grp, logits, NEG_INF), axis=-1, keepdims=True)
    grp = jnp.min(jnp.where(is_grp & (logits == grp_max), lane_f, far), axis=-1, keepdims=True)
    grp_gate = 1.0 / jnp.sum(jnp.where(is_grp, jnp.exp(logits - grp_max), 0.0), axis=-1, keepdims=True)
    lo = N_GROUPS + EXPERTS_PER_GROUP * grp
    in_grp = (lane_f >= lo) & (lane_f < lo + EXPERTS_PER_GROUP)
    v1 = jnp.max(jnp.where(in_grp, logits, NEG_INF), axis=-1, keepdims=True)
    i1 = jnp.min(jnp.where(in_grp & (logits == v1), lane_f, far), axis=-1, keepdims=True)
    rest = in_grp & (lane_f != i1)
    v2 = jnp.max(jnp.where(rest, logits, NEG_INF), axis=-1, keepdims=True)
    i2 = jnp.min(jnp.where(rest & (logits == v2), lane_f, far), axis=-1, keepdims=True)
    t = jnp.exp(v2 - v1)
    g1 = grp_gate / (1.0 + t)
    g2 = g1 * t
    e1 = (i1 - N_GROUPS).astype(jnp.int32)
    e2 = (i2 - N_GROUPS).astype(jnp.int32)
    return (jnp.where(lane == 0, e1, jnp.where(lane == 1, e2, 0)),
            jnp.where(lane == 0, g1, jnp.where(lane == 1, g2, 0.0)))


def _out_router(mix_a, mix_d, x, wo_b, g, b, wr_b, br, layer):
    def row(w):
        return pl.BlockSpec((OUT_TM, w), lambda i: (i, 0))

    return pl.pallas_call(
        _out_router_kernel,
        grid=(N_TOK // OUT_TM,),
        in_specs=[row(ATTN_WIDTH), row(DN_VW), row(D_MODEL), _layer_slab((MIX_WIDTH, D_MODEL), layer),
                  _layer_slab((1, D_MODEL), layer), _layer_slab((1, D_MODEL), layer),
                  _layer_slab((D_MODEL, LANES), layer), _layer_slab((1, LANES), layer)],
        out_specs=[row(D_MODEL), row(LANES), row(LANES)],
        out_shape=[jax.ShapeDtypeStruct((N_TOK, D_MODEL), F32),
                   jax.ShapeDtypeStruct((N_TOK, LANES), jnp.int32),
                   jax.ShapeDtypeStruct((N_TOK, LANES), F32)],
        compiler_params=_params("arbitrary", vmem_limit=VMEM_LIMIT_BIG),
        name="out_router",
    )(mix_a, mix_d, x, wo_b, g, b, wr_b, br)


def _plan_kernel(eid_ref, pos_ref, cnt_ref, run_ref, off_ref):
    phase = pl.program_id(0)
    i = pl.program_id(1)
    lane = lax.broadcasted_iota(jnp.int32, (ROW_TILE, LANES), 1)
    eid = eid_ref[...]
    hot0 = (lane == eid[:, 0:1]).astype(F32)
    hot1 = (lane == eid[:, 1:2]).astype(F32)
    hot = hot0 + hot1
    tile_cnt = jnp.sum(hot, axis=0, keepdims=True)

    @pl.when((phase == 0) & (i == 0))
    def _():
        cnt_ref[...] = jnp.zeros_like(cnt_ref)

    @pl.when(phase == 0)
    def _():
        cnt_ref[...] += tile_cnt

    @pl.when((phase == 1) & (i == 0))
    def _():
        cnt = cnt_ref[...]
        hi = jnp.floor(cnt * (1.0 / 256.0))
        lo = cnt - 256.0 * hi
        r = lax.broadcasted_iota(jnp.int32, (LANES, LANES), 0)
        c = lax.broadcasted_iota(jnp.int32, (LANES, LANES), 1)
        upper = (r < c).astype(F32)
        hi8 = jnp.broadcast_to(hi, (8, LANES))
        lo8 = jnp.broadcast_to(lo, (8, LANES))
        off = 256.0 * _mm(hi8, upper) + _mm(lo8, upper)
        off_ref[...] = off[0:1]
        run_ref[...] = jnp.zeros_like(run_ref)

    @pl.when(phase == 1)
    def _():
        r = lax.broadcasted_iota(jnp.int32, (ROW_TILE, ROW_TILE), 0)
        c = lax.broadcasted_iota(jnp.int32, (ROW_TILE, ROW_TILE), 1)
        before = (r > c).astype(F32)
        slot = _mm(before, hot) + run_ref[...] + off_ref[...]
        p0 = jnp.sum(hot0 * slot, axis=-1, keepdims=True).astype(jnp.int32)
        p1 = jnp.sum(hot1 * slot, axis=-1, keepdims=True).astype(jnp.int32)
        pos_ref[...] = jnp.where(lane == 0, p0, jnp.where(lane == 1, p1, 0))
        run_ref[...] += tile_cnt


def _plan(eid):
    return pl.pallas_call(
        _plan_kernel,
        grid=(2, N_TOK // ROW_TILE),
        in_specs=[pl.BlockSpec((ROW_TILE, LANES), lambda p, i: (i, 0))],
        out_specs=[pl.BlockSpec((ROW_TILE, LANES), lambda p, i: (i * p, 0)),
                   pl.BlockSpec((1, LANES), lambda p, i: (0, 0))],
        out_shape=[jax.ShapeDtypeStruct((N_TOK, LANES), jnp.int32),
                   jax.ShapeDtypeStruct((1, LANES), F32)],
        scratch_shapes=[pltpu.VMEM((1, LANES), F32), pltpu.VMEM((1, LANES), F32)],
        compiler_params=_params("arbitrary", "arbitrary"),
        name="moe_plan",
    )(eid)


def _work_items(counts):
    counts = counts.astype(jnp.int32)
    ends = jnp.cumsum(counts)
    starts = ends - counts
    first_tile = starts // MOE_TM
    n_tiles = jnp.where(counts > 0, (ends - 1) // MOE_TM - first_tile + 1, 0)
    item_end = jnp.cumsum(n_tiles)
    item_start = item_end - n_tiles
    n_items = item_end[-1]
    w = jnp.minimum(jnp.arange(N_MOE_ITEMS, dtype=jnp.int32), n_items - 1)
    expert = jnp.sum((item_end[None, :] <= w[:, None]).astype(jnp.int32), axis=1)
    tile = first_tile[expert] + (w - item_start[expert])
    prev_tile = jnp.concatenate([jnp.full((1,), -1, jnp.int32), tile[:-1]])
    prev_expert = jnp.concatenate([jnp.full((1,), -1, jnp.int32), expert[:-1]])
    valid = (jnp.arange(N_MOE_ITEMS, dtype=jnp.int32) < n_items).astype(jnp.int32)
    return (tile, expert, (tile != prev_tile).astype(jnp.int32), (expert != prev_expert).astype(jnp.int32),
            valid, starts[expert], ends[expert])


def _dispatch_kernel(pos_ref, x_ref, xs_ref, sem):
    base = pl.program_id(0) * (2 * ROW_TILE)

    def row_copy(r, slot):
        return pltpu.make_async_copy(x_ref.at[pl.ds(r, 1), :], xs_ref.at[pl.ds(slot, 1), :], sem)

    def issue(r, carry):
        row_copy(r, pos_ref[base + 2 * r]).start()
        row_copy(r, pos_ref[base + 2 * r + 1]).start()
        return carry

    lax.fori_loop(0, ROW_TILE, issue, 0, unroll=8)
    tile_bytes = pltpu.make_async_copy(x_ref, xs_ref.at[pl.ds(0, ROW_TILE), :], sem)
    tile_bytes.wait()
    tile_bytes.wait()


def _dispatch(pos_flat, x1):
    return pl.pallas_call(
        _dispatch_kernel,
        grid_spec=pltpu.PrefetchScalarGridSpec(
            num_scalar_prefetch=1,
            grid=(N_TOK // ROW_TILE,),
            in_specs=[pl.BlockSpec((ROW_TILE, D_MODEL), lambda i, pos: (i, 0))],
            out_specs=pl.BlockSpec(memory_space=pl.ANY),
            scratch_shapes=[pltpu.SemaphoreType.DMA(())]),
        out_shape=jax.ShapeDtypeStruct((N_ASSIGN, D_MODEL), F32),
        compiler_params=_params("arbitrary"),
        name="moe_dispatch",
    )(pos_flat, x1)


def _experts_kernel(tile_ref, exp_ref, first_ref, newexp_ref, valid_ref, gstart_ref, gend_ref,
                    xs_ref, wgu_ref, wdn_ref, o_ref, wgu_b, wdn_b):
    w = pl.program_id(0)

    @pl.when(valid_ref[w] == 1)
    def _():
        @pl.when(newexp_ref[w] == 1)
        def _():
            wgu_b[...] = wgu_ref[...].astype(BF16)
            wdn_b[...] = wdn_ref[...].astype(BF16)

        h = jnp.dot(xs_ref[...].astype(BF16), wgu_b[...], preferred_element_type=F32)
        act = _silu(h[:, :D_FF_EXPERT]) * h[:, D_FF_EXPERT:]
        y = jnp.dot(act.astype(BF16), wdn_b[...], preferred_element_type=F32)
        rows = tile_ref[w] * MOE_TM + lax.broadcasted_iota(jnp.int32, (MOE_TM, 1), 0)
        mine = (rows >= gstart_ref[w]) & (rows < gend_ref[w])
        y = jnp.where(mine, y, 0.0)

        @pl.when(first_ref[w] == 1)
        def _():
            o_ref[...] = y

        @pl.when(first_ref[w] == 0)
        def _():
            o_ref[...] += y


def _experts(items, xs, w_gu, w_dn, layer):
    return pl.pallas_call(
        _experts_kernel,
        grid_spec=pltpu.PrefetchScalarGridSpec(
            num_scalar_prefetch=7,
            grid=(N_MOE_ITEMS,),
            in_specs=[pl.BlockSpec((MOE_TM, D_MODEL), lambda w, t, e, *_: (t[w], 0)),
                      pl.BlockSpec((None, None, D_MODEL, 2 * D_FF_EXPERT),
                                   lambda w, t, e, *_: (layer, e[w], 0, 0)),
                      pl.BlockSpec((None, None, D_FF_EXPERT, D_MODEL),
                                   lambda w, t, e, *_: (layer, e[w], 0, 0))],
            out_specs=pl.BlockSpec((MOE_TM, D_MODEL), lambda w, t, e, *_: (t[w], 0)),
            scratch_shapes=[pltpu.VMEM((D_MODEL, 2 * D_FF_EXPERT), BF16),
                            pltpu.VMEM((D_FF_EXPERT, D_MODEL), BF16)]),
        out_shape=jax.ShapeDtypeStruct((N_ASSIGN, D_MODEL), F32),
        compiler_params=_params("arbitrary"),
        name="moe_experts",
    )(*items, xs, w_gu, w_dn)


def _combine_kernel(pos_ref, ys_ref, x1_ref, gate_ref, g_ref, b_ref, x2_ref, x2b_ref, r0_ref, r1_ref, sem):
    base = pl.program_id(0) * (2 * ROW_TILE)

    def row_copy(slot, dst, r):
        return pltpu.make_async_copy(ys_ref.at[pl.ds(slot, 1), :], dst.at[pl.ds(r, 1), :], sem)

    def issue(r, carry):
        row_copy(pos_ref[base + 2 * r], r0_ref, r).start()
        row_copy(pos_ref[base + 2 * r + 1], r1_ref, r).start()
        return carry

    lax.fori_loop(0, ROW_TILE, issue, 0, unroll=8)
    pltpu.make_async_copy(ys_ref.at[pl.ds(0, ROW_TILE), :], r0_ref, sem).wait()
    pltpu.make_async_copy(ys_ref.at[pl.ds(0, ROW_TILE), :], r1_ref, sem).wait()
    gate = gate_ref[...]
    y = gate[:, 0:1] * r0_ref[...] + gate[:, 1:2] * r1_ref[...]
    x2 = _layer_norm(DEEPNORM_ALPHA * x1_ref[...] + y, g_ref[...], b_ref[...])
    x2_ref[...] = x2
    x2b_ref[...] = x2.astype(BF16)


def _combine(pos_flat, ys, x1, gates, g, b, layer):
    row = lambda w: pl.BlockSpec((ROW_TILE, w), lambda i, pos: (i, 0))
    vec = _layer_slab((1, D_MODEL), layer)
    return pl.pallas_call(
        _combine_kernel,
        grid_spec=pltpu.PrefetchScalarGridSpec(
            num_scalar_prefetch=1,
            grid=(N_TOK // ROW_TILE,),
            in_specs=[pl.BlockSpec(memory_space=pl.ANY), row(D_MODEL), row(LANES), vec, vec],
            out_specs=[row(D_MODEL), row(D_MODEL)],
            scratch_shapes=[pltpu.VMEM((ROW_TILE, D_MODEL), F32), pltpu.VMEM((ROW_TILE, D_MODEL), F32),
                            pltpu.SemaphoreType.DMA(())]),
        out_shape=[jax.ShapeDtypeStruct((N_TOK, D_MODEL), F32),
                   jax.ShapeDtypeStruct((N_TOK, D_MODEL), BF16)],
        compiler_params=_params("arbitrary"),
        name="moe_combine",
    )(pos_flat, ys, x1, gates, g, b)


def _router_weights(wg, bg, we, be):
    pad = LANES - N_GROUPS - N_EXPERTS
    wr = jnp.concatenate([wg, we, jnp.zeros((DEPTH, D_MODEL, pad), wg.dtype)], axis=-1)
    br = jnp.concatenate([bg, be, jnp.zeros((DEPTH, pad), bg.dtype)], axis=-1)
    return wr.astype(BF16), br.astype(F32).reshape(DEPTH, 1, LANES)


def kernel(x_prompt, x_sample, cache_swa_k, cache_swa_v, state_conv, state_delta, ln_in_g, ln_in_b, w_in, conv_w,
           a_log, dt_bias, dn_norm_w, attn_sinks, w_out, ln1_g, ln1_b, router_group_w, router_group_b,
           router_expert_w, router_expert_b, w_gate_up, w_down, ln2_g, ln2_b):
    x, xb = _ln_in(x_prompt.reshape(N_PROMPT, D_MODEL), x_sample.reshape(N_SAMPLE, D_MODEL), ln_in_g, ln_in_b)
    k_buf = cache_swa_k.reshape(DEPTH, DEC_BATCH, WINDOW, KV_WIDTH)
    v_buf = cache_swa_v.reshape(DEPTH, DEC_BATCH, WINDOW, KV_WIDTH)
    conv_state8 = jnp.pad(state_conv, ((0, 0), (0, 0), (8 - (CONV_W - 1), 0), (0, 0)))
    gp = _gdn_gate_params(a_log, dt_bias)
    norm_w = dn_norm_w.reshape(DEPTH, 1, DN_DV).astype(F32)
    wo_b = w_out.astype(BF16)
    wr_b, br = _router_weights(router_group_w, router_group_b, router_expert_w, router_expert_b)
    ln1_g, ln1_b, ln2_g, ln2_b = (a.reshape(DEPTH, 1, D_MODEL) for a in (ln1_g, ln1_b, ln2_g, ln2_b))
    kp, vp, cp, sp = [], [], [], []
    kq, vq, cq, sq = [], [], [], []
    for l in range(DEPTH):
        proj = _proj(xb, w_in, l)
        sinks = attn_sinks[l].astype(F32)
        o_attn_p = _attn_prompt(proj, sinks)
        o_attn_s, k_s, v_s = _attn_sample(proj, sinks, k_buf, v_buf, l)
        o_dn_p, s_p = _gdn_prompt(proj, conv_w, gp, norm_w, l)
        o_dn_s, s_s = _gdn_sample(proj, conv_w, gp, norm_w, conv_state8, state_delta, GDN_CHUNK_SAMPLE, l)
        mix_a = jnp.concatenate([o_attn_p, o_attn_s.astype(BF16)], axis=0)
        mix_d = jnp.concatenate([o_dn_p, o_dn_s.astype(BF16)], axis=0)
        x1, eid, gates = _out_router(mix_a, mix_d, x, wo_b, ln1_g, ln1_b, wr_b, br, l)
        pos, counts = _plan(eid)
        pos_flat = pos[:, :2].reshape(N_ASSIGN)
        items = _work_items(counts[0, :N_EXPERTS])
        xs = _dispatch(pos_flat, x1)
        ys = _experts(items, xs, w_gate_up, w_down, l)
        x, xb = _combine(pos_flat, ys, x1, gates, ln2_g, ln2_b, l)

        pp = proj[:N_PROMPT].reshape(BATCH, SEQ, IN_COLS)
        ps = proj[N_PROMPT:].reshape(DEC_BATCH, DEC_SEQ, IN_COLS)
        kp.append(pp[:, SEQ - WINDOW:, COL_K:COL_V].reshape(BATCH, WINDOW, KV_HEADS, HEAD_DIM))
        vp.append(pp[:, SEQ - WINDOW:, COL_V:COL_QKV].reshape(BATCH, WINDOW, KV_HEADS, HEAD_DIM))
        cp.append(pp[:, SEQ - (CONV_W - 1):, COL_QKV:COL_Z])
        sp.append(s_p)
        kq.append(k_s.reshape(DEC_BATCH, WINDOW, KV_HEADS, HEAD_DIM))
        vq.append(v_s.reshape(DEC_BATCH, WINDOW, KV_HEADS, HEAD_DIM))
        cq.append(ps[:, DEC_SEQ - (CONV_W - 1):, COL_QKV:COL_Z])
        sq.append(s_s)
    return (x[:N_PROMPT].reshape(BATCH, SEQ, D_MODEL), x[N_PROMPT:].reshape(DEC_BATCH, DEC_SEQ, D_MODEL),
            jnp.stack(kp), jnp.stack(vp), jnp.stack(cp), jnp.stack(sp),
            jnp.stack(kq), jnp.stack(vq), jnp.stack(cq), jnp.stack(sq))
```

```python
import functools

import jax
import jax.numpy as jnp
from jax import lax
from jax.experimental import pallas as pl
from jax.experimental.pallas import tpu as pltpu

D_MODEL = 2048
BATCH = 4
SEQ = 2048
DEPTH = 4
DEC_BATCH = 32
DEC_SEQ = 8
HEAD_DIM = 128
ATTN_HEADS = 8
KV_HEADS = 2
GQA = ATTN_HEADS // KV_HEADS
ATTN_WIDTH = ATTN_HEADS * HEAD_DIM
KV_WIDTH = KV_HEADS * HEAD_DIM
WINDOW = 128
DN_HEADS = 8
DN_DK = 128
DN_DV = 128
DN_KW = DN_HEADS * DN_DK
DN_VW = DN_HEADS * DN_DV
CONV_W = 4
CONV_CH = 2 * DN_KW + DN_VW
MIX_WIDTH = ATTN_WIDTH + DN_VW
IN_COLS = ATTN_WIDTH + 2 * KV_WIDTH + CONV_CH + DN_VW + 2 * DN_HEADS
N_GROUPS = 4
EXPERTS_PER_GROUP = 8
N_EXPERTS = N_GROUPS * EXPERTS_PER_GROUP
D_FF_EXPERT = 256
DEEPNORM_ALPHA = (2 * DEPTH) ** 0.25
LN_EPS = 1e-5
RMS_EPS = 1e-6
NEG_INF = -1e30

N_PROMPT = BATCH * SEQ
N_SAMPLE = DEC_BATCH * DEC_SEQ
N_TOK = N_PROMPT + N_SAMPLE
N_ASSIGN = 2 * N_TOK

COL_K = ATTN_WIDTH
COL_V = COL_K + KV_WIDTH
COL_QKV = COL_V + KV_WIDTH
COL_Z = COL_QKV + CONV_CH
COL_AB = COL_Z + DN_VW

LANES = 128
ROW_TILE = 256
PROJ_TM = 768
PROJ_TN = 1024
GDN_CHUNK = 128
GDN_CHUNK_SAMPLE = 16
GDN_INV_BASE = 16
GDN_COLS = 1536
SAMPLE_BB = 4
MOE_TM = 256
N_MOE_TILES = N_ASSIGN // MOE_TM
N_MOE_ITEMS = N_MOE_TILES + N_EXPERTS - 1
VMEM_LIMIT = 48 * 1024 * 1024

F32 = jnp.float32
BF16 = jnp.bfloat16


def _params(*sem, vmem_limit=VMEM_LIMIT):
    return pltpu.CompilerParams(dimension_semantics=sem, vmem_limit_bytes=vmem_limit)


def _mm(a, b):
    return jnp.dot(a.astype(BF16), b.astype(BF16), preferred_element_type=F32)


def _mm_nt(a, b):
    return lax.dot_general(a.astype(BF16), b.astype(BF16), (((1,), (1,)), ((), ())),
                           preferred_element_type=F32)


def _mm_tn(a, b):
    return lax.dot_general(a.astype(BF16), b.astype(BF16), (((0,), (0,)), ((), ())),
                           preferred_element_type=F32)


def _layer_slab(shape, layer):
    return pl.BlockSpec((None,) + tuple(shape), lambda *_: (layer,) + (0,) * len(shape),
                        pipeline_mode=pl.Buffered(1))


def _layer_norm(x, g, b):
    mu = jnp.mean(x, axis=-1, keepdims=True)
    xc = x - mu
    var = jnp.mean(xc * xc, axis=-1, keepdims=True)
    return xc * lax.rsqrt(var + LN_EPS) * g + b


def _silu(x):
    return x / (1.0 + jnp.exp(-x))


def _ln_in_kernel(xp_ref, xs_ref, g_ref, b_ref, o_ref, ob_ref):
    def emit(x_ref):
        y = _layer_norm(x_ref[...], g_ref[...], b_ref[...])
        o_ref[...] = y
        ob_ref[...] = y.astype(BF16)

    is_prompt = pl.program_id(0) < N_PROMPT // ROW_TILE
    pl.when(is_prompt)(lambda: emit(xp_ref))
    pl.when(jnp.logical_not(is_prompt))(lambda: emit(xs_ref))


def _ln_in(xp, xs, g, b):
    assert N_SAMPLE == ROW_TILE
    row = pl.BlockSpec((ROW_TILE, D_MODEL), lambda i: (i, 0))
    vec = pl.BlockSpec((1, D_MODEL), lambda i: (0, 0))
    last_prompt = N_PROMPT // ROW_TILE - 1
    return pl.pallas_call(
        _ln_in_kernel,
        grid=(N_TOK // ROW_TILE,),
        in_specs=[pl.BlockSpec((ROW_TILE, D_MODEL), lambda i: (jnp.minimum(i, last_prompt), 0)),
                  pl.BlockSpec((ROW_TILE, D_MODEL), lambda i: (0, 0)), vec, vec],
        out_specs=[row, row],
        out_shape=[jax.ShapeDtypeStruct((N_TOK, D_MODEL), F32),
                   jax.ShapeDtypeStruct((N_TOK, D_MODEL), BF16)],
        compiler_params=_params("arbitrary"),
        name="ln_in",
    )(xp, xs, g.reshape(1, D_MODEL), b.reshape(1, D_MODEL))


def _proj_kernel(x_ref, wt_ref, o_ref, wb_ref):
    @pl.when(pl.program_id(1) == 0)
    def _():
        wb_ref[...] = wt_ref[...].astype(BF16)

    o_ref[...] = lax.dot_general(x_ref[...], wb_ref[...], (((1,), (1,)), ((), ())), preferred_element_type=F32)


def _proj(xb, w_in_t, layer):
    return pl.pallas_call(
        _proj_kernel,
        grid=(pl.cdiv(IN_COLS, PROJ_TN), N_TOK // PROJ_TM),
        in_specs=[pl.BlockSpec((PROJ_TM, D_MODEL), lambda j, i: (i, 0)),
                  pl.BlockSpec((None, PROJ_TN, D_MODEL), lambda j, i: (layer, j, 0))],
        out_specs=pl.BlockSpec((PROJ_TM, PROJ_TN), lambda j, i: (i, j)),
        out_shape=jax.ShapeDtypeStruct((N_TOK, IN_COLS), F32),
        scratch_shapes=[pltpu.VMEM((PROJ_TN, D_MODEL), BF16)],
        compiler_params=_params("arbitrary", "arbitrary"),
        name="in_proj",
    )(xb, w_in_t)


def _head_column(values, rows_per_head, n_rows):
    grp = lax.broadcasted_iota(jnp.int32, (n_rows, 1), 0) // rows_per_head
    col = jnp.full((n_rows, 1), values[-1], F32)
    for g in range(len(values) - 2, -1, -1):
        col = jnp.where(grp == g, values[g], col)
    return col


def _softmax_with_sink(s, sink_col):
    m = jnp.maximum(jnp.max(s, axis=-1, keepdims=True), sink_col)
    p = jnp.exp(s - m)
    denom = jnp.sum(p, axis=-1, keepdims=True) + jnp.exp(sink_col - m)
    return p, denom


def _attn_prompt_kernel(sink_ref, q_ref, kc_ref, kp_ref, vc_ref, vp_ref, o_ref):
    blk = pl.program_id(1)
    n_rows = GQA * WINDOW
    qpos = lax.broadcasted_iota(jnp.int32, (n_rows, 2 * WINDOW), 0) % WINDOW
    kidx = lax.broadcasted_iota(jnp.int32, (n_rows, 2 * WINDOW), 1)
    dist_i = qpos + WINDOW - kidx
    valid = (dist_i >= 0) & (dist_i <= WINDOW) & ((kidx >= WINDOW) | (blk > 0))
    dist = dist_i.astype(F32)
    for kvh in range(KV_HEADS):
        cols = slice(kvh * HEAD_DIM, (kvh + 1) * HEAD_DIM)
        kk = jnp.concatenate([kp_ref[:, cols], kc_ref[:, cols]], axis=0)
        vv = jnp.concatenate([vp_ref[:, cols], vc_ref[:, cols]], axis=0)
        heads = [kvh * GQA + g for g in range(GQA)]
        qs = jnp.concatenate([q_ref[:, h * HEAD_DIM:(h + 1) * HEAD_DIM] for h in heads], axis=0)
        slope = _head_column([2.0 ** (-8.0 * (h + 1) / ATTN_HEADS) for h in heads], WINDOW, n_rows)
        sink = _head_column([sink_ref[h] for h in heads], WINDOW, n_rows)
        s = _mm_nt(qs, kk) * (HEAD_DIM ** -0.5) - slope * dist
        s = jnp.where(valid, s, NEG_INF)
        p, denom = _softmax_with_sink(s, sink)
        o = _mm(p, vv) / denom
        for g, h in enumerate(heads):
            o_ref[:, h * HEAD_DIM:(h + 1) * HEAD_DIM] = o[g * WINDOW:(g + 1) * WINDOW].astype(BF16)


def _attn_prompt(proj, sinks):
    nb = SEQ // WINDOW
    kcol, vcol = COL_K // KV_WIDTH, COL_V // KV_WIDTH

    def cur(c):
        return lambda b, i: (b * nb + i, c)

    def prev(c):
        return lambda b, i: (jnp.maximum(b * nb + i - 1, 0), c)

    return pl.pallas_call(
        _attn_prompt_kernel,
        grid=(BATCH, nb),
        in_specs=[pl.BlockSpec(memory_space=pltpu.SMEM),
                  pl.BlockSpec((WINDOW, ATTN_WIDTH), cur(0)),
                  pl.BlockSpec((WINDOW, KV_WIDTH), cur(kcol)),
                  pl.BlockSpec((WINDOW, KV_WIDTH), prev(kcol)),
                  pl.BlockSpec((WINDOW, KV_WIDTH), cur(vcol)),
                  pl.BlockSpec((WINDOW, KV_WIDTH), prev(vcol))],
        out_specs=pl.BlockSpec((WINDOW, ATTN_WIDTH), lambda b, i: (b * nb + i, 0)),
        out_shape=jax.ShapeDtypeStruct((N_PROMPT, ATTN_WIDTH), BF16),
        compiler_params=_params("arbitrary", "arbitrary"),
        name="attn_prompt",
    )(sinks, proj, proj, proj, proj, proj)


def _attn_sample_kernel(sink_ref, q_ref, kn_ref, vn_ref, kb_ref, vb_ref, o_ref, ko_ref, vo_ref):
    n_rows = GQA * DEC_SEQ
    tpos = lax.broadcasted_iota(jnp.int32, (n_rows, 2 * WINDOW), 0) % DEC_SEQ
    kidx = lax.broadcasted_iota(jnp.int32, (n_rows, 2 * WINDOW), 1)
    dist_i = jnp.where(kidx < WINDOW, tpos + WINDOW - kidx, tpos - (kidx - WINDOW))
    valid = (dist_i >= 0) & (dist_i <= WINDOW) & (kidx < WINDOW + DEC_SEQ)
    dist = dist_i.astype(F32)
    pad = jnp.zeros((WINDOW - DEC_SEQ, HEAD_DIM), F32)
    for bb in range(SAMPLE_BB):
        rows = slice(bb * DEC_SEQ, (bb + 1) * DEC_SEQ)
        for kvh in range(KV_HEADS):
            cols = slice(kvh * HEAD_DIM, (kvh + 1) * HEAD_DIM)
            kk = jnp.concatenate([kb_ref[bb, :, cols], kn_ref[rows, cols], pad], axis=0)
            vv = jnp.concatenate([vb_ref[bb, :, cols], vn_ref[rows, cols], pad], axis=0)
            heads = [kvh * GQA + g for g in range(GQA)]
            qs = jnp.concatenate([q_ref[rows, h * HEAD_DIM:(h + 1) * HEAD_DIM] for h in heads], axis=0)
            slope = _head_column([2.0 ** (-8.0 * (h + 1) / ATTN_HEADS) for h in heads], DEC_SEQ, n_rows)
            sink = _head_column([sink_ref[h] for h in heads], DEC_SEQ, n_rows)
            s = _mm_nt(qs, kk) * (HEAD_DIM ** -0.5) - slope * dist
            s = jnp.where(valid, s, NEG_INF)
            p, denom = _softmax_with_sink(s, sink)
            o = _mm(p, vv) / denom
            for g, h in enumerate(heads):
                o_ref[rows, h * HEAD_DIM:(h + 1) * HEAD_DIM] = o[g * DEC_SEQ:(g + 1) * DEC_SEQ]
        ko_ref[bb, 0:WINDOW - DEC_SEQ, :] = kb_ref[bb, DEC_SEQ:WINDOW, :]
        ko_ref[bb, WINDOW - DEC_SEQ:WINDOW, :] = kn_ref[rows, :]
        vo_ref[bb, 0:WINDOW - DEC_SEQ, :] = vb_ref[bb, DEC_SEQ:WINDOW, :]
        vo_ref[bb, WINDOW - DEC_SEQ:WINDOW, :] = vn_ref[rows, :]


def _attn_sample(proj, sinks, k_buf, v_buf, layer):
    rows = SAMPLE_BB * DEC_SEQ
    row0 = N_PROMPT // rows
    kcol, vcol = COL_K // KV_WIDTH, COL_V // KV_WIDTH
    cache_in = pl.BlockSpec((None, SAMPLE_BB, WINDOW, KV_WIDTH), lambda i: (layer, i, 0, 0))
    cache = pl.BlockSpec((SAMPLE_BB, WINDOW, KV_WIDTH), lambda i: (i, 0, 0))
    cache_shape = jax.ShapeDtypeStruct((DEC_BATCH, WINDOW, KV_WIDTH), F32)
    return pl.pallas_call(
        _attn_sample_kernel,
        grid=(DEC_BATCH // SAMPLE_BB,),
        in_specs=[pl.BlockSpec(memory_space=pltpu.SMEM),
                  pl.BlockSpec((rows, ATTN_WIDTH), lambda i: (row0 + i, 0)),
                  pl.BlockSpec((rows, KV_WIDTH), lambda i: (row0 + i, kcol)),
                  pl.BlockSpec((rows, KV_WIDTH), lambda i: (row0 + i, vcol)),
                  cache_in, cache_in],
        out_specs=[pl.BlockSpec((rows, ATTN_WIDTH), lambda i: (i, 0)), cache, cache],
        out_shape=[jax.ShapeDtypeStruct((N_SAMPLE, ATTN_WIDTH), F32), cache_shape, cache_shape],
        compiler_params=_params("arbitrary"),
        name="attn_sample",
    )(sinks, proj, proj, proj, k_buf, v_buf)


def _cumsum_rows(x):
    n = x.shape[0]
    row = lax.broadcasted_iota(jnp.int32, x.shape, 0)
    step = 1
    while step < n:
        x = x + jnp.where(row >= step, pltpu.roll(x, step, axis=0), 0.0)
        step *= 2
    return x


def _gdn_kernel(*refs, chunk, rows_in, has_state):
    if has_state:
        b1_ref, b2_ref, b3_ref, cw_ref, gp_ref, nw_ref, st_ref, s0_ref, o_ref, s_ref, ext_ref = refs
        first = True
    else:
        b1_ref, b2_ref, b3_ref, cw_ref, gp_ref, nw_ref, o_ref, s_ref, ext_ref = refs
        first = pl.program_id(1) == 0

    if has_state:
        ext_ref[0:8, :] = st_ref[...]
        s_ref[...] = s0_ref[...]
    else:
        @pl.when(first)
        def _():
            ext_ref[0:8, :] = jnp.zeros((8, CONV_CH), F32)
            s_ref[...] = jnp.zeros_like(s_ref)

        @pl.when(jnp.logical_not(first))
        def _():
            ext_ref[0:8, :] = ext_ref[chunk:chunk + 8, :]

    ext_ref[8:8 + rows_in, 0:GDN_COLS] = b1_ref[...]
    ext_ref[8:8 + rows_in, GDN_COLS:CONV_CH] = b2_ref[...]
    if rows_in < chunk:
        ext_ref[8 + rows_in:8 + chunk, :] = jnp.zeros((chunk - rows_in, CONV_CH), F32)

    lane = lax.broadcasted_iota(jnp.int32, (chunk, LANES), 1)
    row = lax.broadcasted_iota(jnp.int32, (chunk, LANES), 0)
    if rows_in < chunk:
        ab = jnp.concatenate([b3_ref[:, DN_VW:DN_VW + LANES],
                              jnp.zeros((chunk - rows_in, LANES), F32)], axis=0)
    else:
        ab = b3_ref[:, DN_VW:DN_VW + LANES]
    live = (lane < 2 * DN_HEADS) & (row < rows_in)
    ab = jnp.where(live, ab, 0.0)
    a_scale = gp_ref[0:1, :]
    dt_bias = gp_ref[1:2, :]
    g_all = jnp.where(live, a_scale * jax.nn.softplus(ab + dt_bias), 0.0)
    beta_all = jnp.where(live, jax.nn.sigmoid(ab), 0.0)
    gcum = _cumsum_rows(g_all)
    gcum_t = gcum.T
    g_last_all = gcum[chunk - 1:chunk, :]

    ri = lax.broadcasted_iota(jnp.int32, (chunk, chunk), 0)
    ci = lax.broadcasted_iota(jnp.int32, (chunk, chunk), 1)
    incl = ri >= ci
    strict = ri > ci

    def conv_strip(col0):
        cols = slice(col0, col0 + LANES)
        acc = ext_ref[8:8 + chunk, cols] * cw_ref[CONV_W - 1:CONV_W, cols]
        for tap in range(1, CONV_W):
            acc = acc + ext_ref[8 - tap:8 - tap + chunk, cols] * cw_ref[CONV_W - 1 - tap:CONV_W - tap, cols]
        return _silu(acc)

    def l2norm(x):
        return x * lax.rsqrt(jnp.sum(x * x, axis=-1, keepdims=True) + RMS_EPS)

    heads = range(DN_HEADS)
    q = [l2norm(conv_strip(h * DN_DK)) * (DN_DK ** -0.5) for h in heads]
    k = [l2norm(conv_strip(DN_KW + h * DN_DK)) for h in heads]
    v = [conv_strip(2 * DN_KW + h * DN_DV) for h in heads]
    gc_col = [gcum[:, h:h + 1] for h in heads]
    g_last = [g_last_all[:, h:h + 1] for h in heads]
    beta = [beta_all[:, DN_HEADS + h:DN_HEADS + h + 1] for h in heads]
    decay = [jnp.where(incl, jnp.exp(jnp.where(incl, gc_col[h] - gcum_t[h:h + 1, :], 0.0)), 0.0) for h in heads]
    eg = [jnp.exp(gc_col[h]) for h in heads]
    kb = [k[h] * beta[h] for h in heads]
    kk = [_mm_nt(jnp.concatenate([kb[h], q[h]], axis=0), k[h]) for h in heads]
    qk = [jnp.where(incl, kk[h][chunk:] * decay[h], 0.0) for h in heads]
    neg_a = [jnp.where(strict, -kk[h][:chunk] * decay[h], 0.0) for h in heads]
    base = min(GDN_INV_BASE, chunk)
    p = [jnp.where(ri // base == ci // base, neg_a[h], 0.0) for h in heads]
    eye = (ri == ci).astype(F32)
    inv = [eye + p[h] for h in heads]
    span = 2
    while span < base:
        p = [_mm(p[h], p[h]) for h in heads]
        inv = [inv[h] + _mm(inv[h], p[h]) for h in heads]
        span *= 2
    size = base
    while size < chunk:
        lower_left = (ri // (2 * size) == ci // (2 * size)) & (ri // size != ci // size)
        off = [jnp.where(lower_left, neg_a[h], 0.0) for h in heads]
        inv = [inv[h] + _mm(_mm(inv[h], off[h]), inv[h]) for h in heads]
        size *= 2
    sol = [_mm(inv[h], jnp.concatenate([v[h] * beta[h], kb[h] * eg[h]], axis=1)) for h in heads]
    s_prev = [s_ref[h] for h in heads]
    ws = [_mm(jnp.concatenate([sol[h][:, DN_DV:], q[h] * eg[h]], axis=0), s_prev[h]) for h in heads]
    v_new = [sol[h][:, :DN_DV] - ws[h][:chunk] for h in heads]
    o = [ws[h][chunk:] + _mm(qk[h], v_new[h]) for h in heads]
    k_dec = [k[h] * jnp.exp(g_last[h] - gc_col[h]) for h in heads]
    for h in heads:
        s_ref[h] = s_prev[h] * jnp.exp(g_last[h]) + _mm_tn(k_dec[h], v_new[h])
    for h in heads:
        on = o[h] * lax.rsqrt(jnp.mean(o[h] * o[h], axis=-1, keepdims=True) + RMS_EPS)
        z = b3_ref[:, h * DN_DV:(h + 1) * DN_DV]
        out = on[0:rows_in] * nw_ref[...] * _silu(z)
        o_ref[:, h * DN_DV:(h + 1) * DN_DV] = out.astype(o_ref.dtype)


def _gdn_gate_params(a_log, dt_bias):
    rows = jnp.stack([-jnp.exp(a_log.astype(F32)), dt_bias.astype(F32)], axis=1)
    return jnp.pad(rows, ((0, 0), (0, 0), (0, LANES - DN_HEADS)))


def _gdn_prompt(proj, conv_w, gp, norm_w, layer):
    nc = SEQ // GDN_CHUNK
    c0 = COL_QKV // GDN_COLS

    def win(c):
        return pl.BlockSpec((GDN_CHUNK, GDN_COLS), lambda b, i: (b * nc + i, c))

    return pl.pallas_call(
        functools.partial(_gdn_kernel, chunk=GDN_CHUNK, rows_in=GDN_CHUNK, has_state=False),
        grid=(BATCH, nc),
        in_specs=[win(c0), win(c0 + 1), win(c0 + 2), _layer_slab((CONV_W, CONV_CH), layer),
                  _layer_slab((2, LANES), layer), _layer_slab((1, DN_DV), layer)],
        out_specs=[pl.BlockSpec((GDN_CHUNK, DN_VW), lambda b, i: (b * nc + i, 0)),
                   pl.BlockSpec((None, DN_HEADS, DN_DK, DN_DV), lambda b, i: (b, 0, 0, 0))],
        out_shape=[jax.ShapeDtypeStruct((N_PROMPT, DN_VW), BF16),
                   jax.ShapeDtypeStruct((BATCH, DN_HEADS, DN_DK, DN_DV), F32)],
        scratch_shapes=[pltpu.VMEM((8 + GDN_CHUNK, CONV_CH), F32)],
        compiler_params=_params("arbitrary", "arbitrary"),
        name="gdn_prompt",
    )(proj, proj, proj, conv_w, gp, norm_w)


def _gdn_sample(proj, conv_w, gp, norm_w, conv_state8, s0, chunk, layer):
    row0 = N_PROMPT // DEC_SEQ
    c0 = COL_QKV // GDN_COLS

    def win(c):
        return pl.BlockSpec((DEC_SEQ, GDN_COLS), lambda b: (row0 + b, c))

    state = pl.BlockSpec((None, DN_HEADS, DN_DK, DN_DV), lambda b: (b, 0, 0, 0))
    return pl.pallas_call(
        functools.partial(_gdn_kernel, chunk=chunk, rows_in=DEC_SEQ, has_state=True),
        grid=(DEC_BATCH,),
        in_specs=[win(c0), win(c0 + 1), win(c0 + 2), _layer_slab((CONV_W, CONV_CH), layer),
                  _layer_slab((2, LANES), layer), _layer_slab((1, DN_DV), layer),
                  pl.BlockSpec((None, None, 8, CONV_CH), lambda b: (layer, b, 0, 0)),
                  pl.BlockSpec((None, None, DN_HEADS, DN_DK, DN_DV), lambda b: (layer, b, 0, 0, 0))],
        out_specs=[pl.BlockSpec((DEC_SEQ, DN_VW), lambda b: (b, 0)), state],
        out_shape=[jax.ShapeDtypeStruct((N_SAMPLE, DN_VW), F32),
                   jax.ShapeDtypeStruct((DEC_BATCH, DN_HEADS, DN_DK, DN_DV), F32)],
        scratch_shapes=[pltpu.VMEM((8 + chunk, CONV_CH), F32)],
        compiler_params=_params("arbitrary"),
        name="gdn_sample",
    )(proj, proj, proj, conv_w, gp, norm_w, conv_state8, s0)


def _out_router_kernel(map_ref, mas_ref, mdp_ref, mds_ref, x_ref, wo_ref, g_ref, b_ref, wr_ref, br_ref,
                       x1_ref, eid_ref, gate_ref):
    def emit(ma_ref, md_ref):
        mix = (jnp.dot(ma_ref[...].astype(BF16), wo_ref[0:ATTN_WIDTH, :], preferred_element_type=F32)
               + jnp.dot(md_ref[...].astype(BF16), wo_ref[ATTN_WIDTH:MIX_WIDTH, :], preferred_element_type=F32))
        x1 = _layer_norm(DEEPNORM_ALPHA * x_ref[...] + mix, g_ref[...], b_ref[...])
        x1_ref[...] = x1
        logits = jnp.dot(x1.astype(BF16), wr_ref[...], preferred_element_type=F32) + br_ref[...]
        eid, gate = _route(logits)
        eid_ref[...] = eid
        gate_ref[...] = gate

    is_prompt = pl.program_id(0) < N_PROMPT // ROW_TILE
    pl.when(is_prompt)(lambda: emit(map_ref, mdp_ref))
    pl.when(jnp.logical_not(is_prompt))(lambda: emit(mas_ref, mds_ref))


def _route(logits):
    lane = lax.broadcasted_iota(jnp.int32, logits.shape, 1)
    lane_f = lane.astype(F32)
    far = float(LANES)
    is_grp = lane < N_GROUPS
    grp_max = jnp.max(jnp.where(is_grp, logits, NEG_INF), axis=-1, keepdims=True)
    grp = jnp.min(jnp.where(is_grp & (logits == grp_max), lane_f, far), axis=-1, keepdims=True)
    grp_gate = 1.0 / jnp.sum(jnp.where(is_grp, jnp.exp(logits - grp_max), 0.0), axis=-1, keepdims=True)
    lo = N_GROUPS + EXPERTS_PER_GROUP * grp
    in_grp = (lane_f >= lo) & (lane_f < lo + EXPERTS_PER_GROUP)
    v1 = jnp.max(jnp.where(in_grp, logits, NEG_INF), axis=-1, keepdims=True)
    i1 = jnp.min(jnp.where(in_grp & (logits == v1), lane_f, far), axis=-1, keepdims=True)
    rest = in_grp & (lane_f != i1)
    v2 = jnp.max(jnp.where(rest, logits, NEG_INF), axis=-1, keepdims=True)
    i2 = jnp.min(jnp.where(rest & (logits == v2), lane_f, far), axis=-1, keepdims=True)
    t = jnp.exp(v2 - v1)
    g1 = grp_gate / (1.0 + t)
    g2 = g1 * t
    e1 = (i1 - N_GROUPS).astype(jnp.int32)
    e2 = (i2 - N_GROUPS).astype(jnp.int32)
    return (jnp.where(lane == 0, e1, jnp.where(lane == 1, e2, 0)),
            jnp.where(lane == 0, g1, jnp.where(lane == 1, g2, 0.0)))


def _out_router(mix_a_p, mix_a_s, mix_d_p, mix_d_s, x, wo_b, g, b, wr_b, br, layer):
    assert N_SAMPLE == ROW_TILE

    def row(w):
        return pl.BlockSpec((ROW_TILE, w), lambda i: (i, 0))

    last_prompt = N_PROMPT // ROW_TILE - 1
    prompt = pl.BlockSpec((ROW_TILE, ATTN_WIDTH), lambda i: (jnp.minimum(i, last_prompt), 0))
    sample = pl.BlockSpec((ROW_TILE, ATTN_WIDTH), lambda i: (0, 0))
    return pl.pallas_call(
        _out_router_kernel,
        grid=(N_TOK // ROW_TILE,),
        in_specs=[prompt, sample, prompt, sample, row(D_MODEL), _layer_slab((MIX_WIDTH, D_MODEL), layer),
                  _layer_slab((1, D_MODEL), layer), _layer_slab((1, D_MODEL), layer),
                  _layer_slab((D_MODEL, LANES), layer), _layer_slab((1, LANES), layer)],
        out_specs=[row(D_MODEL), row(LANES), row(LANES)],
        out_shape=[jax.ShapeDtypeStruct((N_TOK, D_MODEL), F32),
                   jax.ShapeDtypeStruct((N_TOK, LANES), jnp.int32),
                   jax.ShapeDtypeStruct((N_TOK, LANES), F32)],
        compiler_params=_params("arbitrary"),
        name="out_router",
    )(mix_a_p, mix_a_s, mix_d_p, mix_d_s, x, wo_b, g, b, wr_b, br)


def _plan_kernel(eid_ref, pos_ref, cnt_ref, run_ref, off_ref):
    phase = pl.program_id(0)
    i = pl.program_id(1)
    lane = lax.broadcasted_iota(jnp.int32, (ROW_TILE, LANES), 1)
    eid = eid_ref[...]
    hot0 = (lane == eid[:, 0:1]).astype(F32)
    hot1 = (lane == eid[:, 1:2]).astype(F32)
    hot = hot0 + hot1
    tile_cnt = jnp.sum(hot, axis=0, keepdims=True)

    @pl.when((phase == 0) & (i == 0))
    def _():
        cnt_ref[...] = jnp.zeros_like(cnt_ref)

    @pl.when(phase == 0)
    def _():
        cnt_ref[...] += tile_cnt

    @pl.when((phase == 1) & (i == 0))
    def _():
        cnt = cnt_ref[...]
        hi = jnp.floor(cnt * (1.0 / 256.0))
        lo = cnt - 256.0 * hi
        r = lax.broadcasted_iota(jnp.int32, (LANES, LANES), 0)
        c = lax.broadcasted_iota(jnp.int32, (LANES, LANES), 1)
        upper = (r < c).astype(F32)
        hi8 = jnp.broadcast_to(hi, (8, LANES))
        lo8 = jnp.broadcast_to(lo, (8, LANES))
        off = 256.0 * _mm(hi8, upper) + _mm(lo8, upper)
        off_ref[...] = off[0:1]
        run_ref[...] = jnp.zeros_like(run_ref)

    @pl.when(phase == 1)
    def _():
        r = lax.broadcasted_iota(jnp.int32, (ROW_TILE, ROW_TILE), 0)
        c = lax.broadcasted_iota(jnp.int32, (ROW_TILE, ROW_TILE), 1)
        before = (r > c).astype(F32)
        slot = _mm(before, hot) + run_ref[...] + off_ref[...]
        p0 = jnp.sum(hot0 * slot, axis=-1, keepdims=True).astype(jnp.int32)
        p1 = jnp.sum(hot1 * slot, axis=-1, keepdims=True).astype(jnp.int32)
        pos_ref[...] = jnp.where(lane == 0, p0, jnp.where(lane == 1, p1, 0))
        run_ref[...] += tile_cnt


def _plan(eid):
    return pl.pallas_call(
        _plan_kernel,
        grid=(2, N_TOK // ROW_TILE),
        in_specs=[pl.BlockSpec((ROW_TILE, LANES), lambda p, i: (i, 0))],
        out_specs=[pl.BlockSpec((ROW_TILE, LANES), lambda p, i: (i * p, 0)),
                   pl.BlockSpec((1, LANES), lambda p, i: (0, 0))],
        out_shape=[jax.ShapeDtypeStruct((N_TOK, LANES), jnp.int32),
                   jax.ShapeDtypeStruct((1, LANES), F32)],
        scratch_shapes=[pltpu.VMEM((1, LANES), F32), pltpu.VMEM((1, LANES), F32)],
        compiler_params=_params("arbitrary", "arbitrary"),
        name="moe_plan",
    )(eid)


def _work_items(counts):
    counts = counts.astype(jnp.int32)
    ends = jnp.cumsum(counts)
    starts = ends - counts
    first_tile = starts // MOE_TM
    n_tiles = jnp.where(counts > 0, (ends - 1) // MOE_TM - first_tile + 1, 0)
    item_end = jnp.cumsum(n_tiles)
    item_start = item_end - n_tiles
    n_items = item_end[-1]
    w = jnp.minimum(jnp.arange(N_MOE_ITEMS, dtype=jnp.int32), n_items - 1)
    expert = jnp.sum((item_end[None, :] <= w[:, None]).astype(jnp.int32), axis=1)
    tile = first_tile[expert] + (w - item_start[expert])
    prev_tile = jnp.concatenate([jnp.full((1,), -1, jnp.int32), tile[:-1]])
    prev_expert = jnp.concatenate([jnp.full((1,), -1, jnp.int32), expert[:-1]])
    valid = (jnp.arange(N_MOE_ITEMS, dtype=jnp.int32) < n_items).astype(jnp.int32)
    return (tile, expert, (tile != prev_tile).astype(jnp.int32), (expert != prev_expert).astype(jnp.int32),
            valid, starts[expert], ends[expert])


def _dispatch_kernel(pos_ref, x_ref, xs_ref, sem):
    base = pl.program_id(0) * (2 * ROW_TILE)

    def row_copy(r, slot):
        return pltpu.make_async_copy(x_ref.at[pl.ds(r, 1), :], xs_ref.at[pl.ds(slot, 1), :], sem)

    def issue(r, carry):
        row_copy(r, pos_ref[base + 2 * r]).start()
        row_copy(r, pos_ref[base + 2 * r + 1]).start()
        return carry

    lax.fori_loop(0, ROW_TILE, issue, 0, unroll=8)
    tile_bytes = pltpu.make_async_copy(x_ref, xs_ref.at[pl.ds(0, ROW_TILE), :], sem)
    tile_bytes.wait()
    tile_bytes.wait()


def _dispatch(pos_flat, x1):
    width = x1.shape[1]
    return pl.pallas_call(
        _dispatch_kernel,
        grid_spec=pltpu.PrefetchScalarGridSpec(
            num_scalar_prefetch=1,
            grid=(N_TOK // ROW_TILE,),
            in_specs=[pl.BlockSpec((ROW_TILE, width), lambda i, pos: (i, 0))],
            out_specs=pl.BlockSpec(memory_space=pl.ANY),
            scratch_shapes=[pltpu.SemaphoreType.DMA(())]),
        out_shape=jax.ShapeDtypeStruct((N_ASSIGN, width), x1.dtype),
        compiler_params=_params("arbitrary"),
        name="moe_dispatch",
    )(pos_flat, x1)


def _experts_kernel(tile_ref, exp_ref, first_ref, newexp_ref, valid_ref, gstart_ref, gend_ref,
                    xs_ref, wgu_ref, wdn_ref, o_ref, wgu_b, wdn_b):
    w = pl.program_id(0)

    @pl.when(valid_ref[w] == 1)
    def _():
        @pl.when(newexp_ref[w] == 1)
        def _():
            wgu_b[...] = wgu_ref[...].astype(BF16)
            wdn_b[...] = wdn_ref[...].astype(BF16)

        h = jnp.dot(xs_ref[...].astype(BF16), wgu_b[...], preferred_element_type=F32)
        act = _silu(h[:, :D_FF_EXPERT]) * h[:, D_FF_EXPERT:]
        y = jnp.dot(act.astype(BF16), wdn_b[...], preferred_element_type=F32)
        rows = tile_ref[w] * MOE_TM + lax.broadcasted_iota(jnp.int32, (MOE_TM, 1), 0)
        mine = (rows >= gstart_ref[w]) & (rows < gend_ref[w])

        @pl.when(first_ref[w] == 1)
        def _():
            o_ref[...] = jnp.where(mine, y, 0.0)

        @pl.when(first_ref[w] == 0)
        def _():
            o_ref[...] = jnp.where(mine, y, o_ref[...])


def _experts(items, xs, w_gu, w_dn, layer):
    return pl.pallas_call(
        _experts_kernel,
        grid_spec=pltpu.PrefetchScalarGridSpec(
            num_scalar_prefetch=7,
            grid=(N_MOE_ITEMS,),
            in_specs=[pl.BlockSpec((MOE_TM, D_MODEL), lambda w, t, e, *_: (t[w], 0)),
                      pl.BlockSpec((None, None, D_MODEL, 2 * D_FF_EXPERT),
                                   lambda w, t, e, *_: (layer, e[w], 0, 0)),
                      pl.BlockSpec((None, None, D_FF_EXPERT, D_MODEL),
                                   lambda w, t, e, *_: (layer, e[w], 0, 0))],
            out_specs=pl.BlockSpec((MOE_TM, D_MODEL), lambda w, t, e, *_: (t[w], 0)),
            scratch_shapes=[pltpu.VMEM((D_MODEL, 2 * D_FF_EXPERT), BF16),
                            pltpu.VMEM((D_FF_EXPERT, D_MODEL), BF16)]),
        out_shape=jax.ShapeDtypeStruct((N_ASSIGN, D_MODEL), F32),
        compiler_params=_params("arbitrary"),
        name="moe_experts",
    )(*items, xs, w_gu, w_dn)


def _combine_kernel(pos_ref, ys_ref, x1_ref, gate_ref, g_ref, b_ref, x2_ref, x2b_ref, rows_ref, sem):
    i = pl.program_id(0)
    buf = i % 2

    def start_gather(tile, into):
        base = tile * (2 * ROW_TILE)

        def issue(r, carry):
            for pick in range(2):
                pltpu.make_async_copy(ys_ref.at[pl.ds(pos_ref[base + 2 * r + pick], 1), :],
                                      rows_ref.at[into, pick, pl.ds(r, 1), :], sem.at[into]).start()
            return carry

        lax.fori_loop(0, ROW_TILE, issue, 0, unroll=8)

    @pl.when(i == 0)
    def _():
        start_gather(0, 0)

    @pl.when(i + 1 < pl.num_programs(0))
    def _():
        start_gather(i + 1, 1 - buf)

    for pick in range(2):
        pltpu.make_async_copy(ys_ref.at[pl.ds(0, ROW_TILE), :], rows_ref.at[buf, pick], sem.at[buf]).wait()
    gate = gate_ref[...]
    y = gate[:, 0:1] * rows_ref[buf, 0] + gate[:, 1:2] * rows_ref[buf, 1]
    x2 =_layer_norm(DEEPNORM_ALPHA * x1_ref[...] + y, g_ref[...], b_ref[...])
    x2_ref[...] = x2
    x2b_ref[...] = x2.astype(BF16)


def _combine(pos_flat, ys, x1, gates, g, b, layer):
    row = lambda w: pl.BlockSpec((ROW_TILE, w), lambda i, pos: (i, 0))
    vec = _layer_slab((1, D_MODEL), layer)
    return pl.pallas_call(
        _combine_kernel,
        grid_spec=pltpu.PrefetchScalarGridSpec(
            num_scalar_prefetch=1,
            grid=(N_TOK // ROW_TILE,),
            in_specs=[pl.BlockSpec(memory_space=pl.ANY), row(D_MODEL), row(LANES), vec, vec],
            out_specs=[row(D_MODEL), row(D_MODEL)],
            scratch_shapes=[pltpu.VMEM((2, 2, ROW_TILE, D_MODEL), F32),
                            pltpu.SemaphoreType.DMA((2,))]),
        out_shape=[jax.ShapeDtypeStruct((N_TOK, D_MODEL), F32),
                   jax.ShapeDtypeStruct((N_TOK, D_MODEL), BF16)],
        compiler_params=_params("arbitrary"),
        name="moe_combine",
    )(pos_flat, ys, x1, gates, g, b)


def _router_weights(wg, bg, we, be):
    pad = LANES - N_GROUPS - N_EXPERTS
    wr = jnp.concatenate([wg, we, jnp.zeros((DEPTH, D_MODEL, pad), wg.dtype)], axis=-1)
    br = jnp.concatenate([bg, be, jnp.zeros((DEPTH, pad), bg.dtype)], axis=-1)
    return wr.astype(BF16), br.astype(F32).reshape(DEPTH, 1, LANES)


def kernel(x_prompt, x_sample, cache_swa_k, cache_swa_v, state_conv, state_delta, ln_in_g, ln_in_b, w_in, conv_w,
           a_log, dt_bias, dn_norm_w, attn_sinks, w_out, ln1_g, ln1_b, router_group_w, router_group_b,
           router_expert_w, router_expert_b, w_gate_up, w_down, ln2_g, ln2_b):
    x, xb = _ln_in(x_prompt.reshape(N_PROMPT, D_MODEL), x_sample.reshape(N_SAMPLE, D_MODEL), ln_in_g, ln_in_b)
    k_buf = cache_swa_k.reshape(DEPTH, DEC_BATCH, WINDOW, KV_WIDTH)
    v_buf = cache_swa_v.reshape(DEPTH, DEC_BATCH, WINDOW, KV_WIDTH)
    conv_state8 = jnp.pad(state_conv, ((0, 0), (0, 0), (8 - (CONV_W - 1), 0), (0, 0)))
    gp = _gdn_gate_params(a_log, dt_bias)
    norm_w = dn_norm_w.reshape(DEPTH, 1, DN_DV).astype(F32)
    wo_b = w_out.astype(BF16)
    w_in_t = jnp.swapaxes(w_in, 1, 2)
    wr_b, br = _router_weights(router_group_w, router_group_b, router_expert_w, router_expert_b)
    ln1_g, ln1_b, ln2_g, ln2_b = (a.reshape(DEPTH, 1, D_MODEL) for a in (ln1_g, ln1_b, ln2_g, ln2_b))
    kp, vp, cp, sp = [], [], [], []
    kq, vq, cq, sq = [], [], [], []
    for l in range(DEPTH):
        proj = _proj(xb, w_in_t, l)
        sinks = attn_sinks[l].astype(F32)
        o_attn_p = _attn_prompt(proj, sinks)
        o_attn_s, k_s, v_s = _attn_sample(proj, sinks, k_buf, v_buf, l)
        o_dn_p, s_p = _gdn_prompt(proj, conv_w, gp, norm_w, l)
        o_dn_s, s_s = _gdn_sample(proj, conv_w, gp, norm_w, conv_state8, state_delta, GDN_CHUNK_SAMPLE, l)
        x1, eid, gates = _out_router(o_attn_p, o_attn_s, o_dn_p, o_dn_s, x, wo_b, ln1_g, ln1_b, wr_b, br, l)
        pos, counts = _plan(eid)
        pos_flat = pos[:, :2].reshape(N_ASSIGN)
        items = _work_items(counts[0, :N_EXPERTS])
        xs = _dispatch(pos_flat, x1)
        ys = _experts(items, xs, w_gate_up, w_down, l)
        x, xb = _combine(pos_flat, ys, x1, gates, ln2_g, ln2_b, l)

        def prompt_tail(rows, c0, c1):
            return jnp.stack([proj[(b + 1) * SEQ - rows:(b + 1) * SEQ, c0:c1] for b in range(BATCH)])

        kp.append(prompt_tail(WINDOW, COL_K, COL_V).reshape(BATCH, WINDOW, KV_HEADS, HEAD_DIM))
        vp.append(prompt_tail(WINDOW, COL_V, COL_QKV).reshape(BATCH, WINDOW, KV_HEADS, HEAD_DIM))
        cp.append(prompt_tail(CONV_W - 1, COL_QKV, COL_Z))
        sp.append(s_p)
        kq.append(k_s.reshape(DEC_BATCH, WINDOW, KV_HEADS, HEAD_DIM))
        vq.append(v_s.reshape(DEC_BATCH, WINDOW, KV_HEADS, HEAD_DIM))
        cq.append(proj[N_PROMPT:, COL_QKV:COL_Z].reshape(DEC_BATCH, DEC_SEQ, CONV_CH)[:, DEC_SEQ - (CONV_W - 1):])
        sq.append(s_s)
    return (x[:N_PROMPT].reshape(BATCH, SEQ, D_MODEL), x[N_PROMPT:].reshape(DEC_BATCH, DEC_SEQ, D_MODEL),
            jnp.stack(kp), jnp.stack(vp), jnp.stack(cp), jnp.stack(sp),
            jnp.stack(kq), jnp.stack(vq), jnp.stack(cq), jnp.stack(sq))
```

```python
import functools

import jax
import jax.numpy as jnp
from jax import lax
from jax.experimental import pallas as pl
from jax.experimental.pallas import tpu as pltpu

D_MODEL = 2048
BATCH = 4
SEQ = 2048
DEPTH = 4
DEC_BATCH = 32
DEC_SEQ = 8
HEAD_DIM = 128
ATTN_HEADS = 8
KV_HEADS = 2
GQA = ATTN_HEADS // KV_HEADS
ATTN_WIDTH = ATTN_HEADS * HEAD_DIM
KV_WIDTH = KV_HEADS * HEAD_DIM
WINDOW = 128
DN_HEADS = 8
DN_DK = 128
DN_DV = 128
DN_KW = DN_HEADS * DN_DK
DN_VW = DN_HEADS * DN_DV
CONV_W = 4
CONV_CH = 2 * DN_KW + DN_VW
MIX_WIDTH = ATTN_WIDTH + DN_VW
IN_COLS = ATTN_WIDTH + 2 * KV_WIDTH + CONV_CH + DN_VW + 2 * DN_HEADS
N_GROUPS = 4
EXPERTS_PER_GROUP = 8
N_EXPERTS = N_GROUPS * EXPERTS_PER_GROUP
D_FF_EXPERT = 256
DEEPNORM_ALPHA = (2 * DEPTH) ** 0.25
LN_EPS = 1e-5
RMS_EPS = 1e-6
NEG_INF = -1e30

N_PROMPT = BATCH * SEQ
N_SAMPLE = DEC_BATCH * DEC_SEQ
N_TOK = N_PROMPT + N_SAMPLE
N_ASSIGN = 2 * N_TOK

COL_K = ATTN_WIDTH
COL_V = COL_K + KV_WIDTH
COL_QKV = COL_V + KV_WIDTH
COL_Z = COL_QKV + CONV_CH
COL_AB = COL_Z + DN_VW

LANES = 128
ROW_TILE = 256
PROJ_TM = 1408
PROJ_TN = 1024
GDN_CHUNK = 128
GDN_CHUNK_SAMPLE = 16
GDN_INV_BASE = 16
GDN_COLS = 1536
SAMPLE_BB = 4
MOE_TM = 256
N_MOE_TILES = N_ASSIGN // MOE_TM
N_MOE_ITEMS = N_MOE_TILES + N_EXPERTS - 1
VMEM_LIMIT = 48 * 1024 * 1024
VMEM_LIMIT_BIG = 56 * 1024 * 1024

F32 = jnp.float32
BF16 = jnp.bfloat16


def _params(*sem, vmem_limit=VMEM_LIMIT):
    return pltpu.CompilerParams(dimension_semantics=sem, vmem_limit_bytes=vmem_limit)


def _mm(a, b):
    return jnp.dot(a.astype(BF16), b.astype(BF16), preferred_element_type=F32)


def _mm_nt(a, b):
    return lax.dot_general(a.astype(BF16), b.astype(BF16), (((1,), (1,)), ((), ())),
                           preferred_element_type=F32)


def _mm_tn(a, b):
    return lax.dot_general(a.astype(BF16), b.astype(BF16), (((0,), (0,)), ((), ())),
                           preferred_element_type=F32)


def _layer_slab(shape, layer):
    return pl.BlockSpec((None,) + tuple(shape), lambda *_: (layer,) + (0,) * len(shape),
                        pipeline_mode=pl.Buffered(1))


def _layer_norm(x, g, b):
    mu = jnp.mean(x, axis=-1, keepdims=True)
    xc = x - mu
    var = jnp.mean(xc * xc, axis=-1, keepdims=True)
    return xc * lax.rsqrt(var + LN_EPS) * g + b


def _silu(x):
    return x / (1.0 + jnp.exp(-x))


def _ln_in_kernel(xp_ref, xs_ref, g_ref, b_ref, o_ref, ob_ref):
    def emit(x_ref):
        y = _layer_norm(x_ref[...], g_ref[...], b_ref[...])
        o_ref[...] = y
        ob_ref[...] = y.astype(BF16)

    is_prompt = pl.program_id(0) < N_PROMPT // ROW_TILE
    pl.when(is_prompt)(lambda: emit(xp_ref))
    pl.when(jnp.logical_not(is_prompt))(lambda: emit(xs_ref))


def _ln_in(xp, xs, g, b):
    assert N_SAMPLE == ROW_TILE
    row = pl.BlockSpec((ROW_TILE, D_MODEL), lambda i: (i, 0))
    vec = pl.BlockSpec((1, D_MODEL), lambda i: (0, 0))
    last_prompt = N_PROMPT // ROW_TILE - 1
    return pl.pallas_call(
        _ln_in_kernel,
        grid=(N_TOK // ROW_TILE,),
        in_specs=[pl.BlockSpec((ROW_TILE, D_MODEL), lambda i: (jnp.minimum(i, last_prompt), 0)),
                  pl.BlockSpec((ROW_TILE, D_MODEL), lambda i: (0, 0)), vec, vec],
        out_specs=[row, row],
        out_shape=[jax.ShapeDtypeStruct((N_TOK, D_MODEL), F32),
                   jax.ShapeDtypeStruct((N_TOK, D_MODEL), BF16)],
        compiler_params=_params("arbitrary"),
        name="ln_in",
    )(xp, xs, g.reshape(1, D_MODEL), b.reshape(1, D_MODEL))


def _proj_kernel(x_ref, wt_ref, o_ref, wb_ref):
    @pl.when(pl.program_id(1) == 0)
    def _():
        wb_ref[...] = wt_ref[...].astype(BF16)

    o_ref[...] = lax.dot_general(x_ref[...], wb_ref[...], (((1,), (1,)), ((), ())), preferred_element_type=F32)


def _proj(xb, w_in_t, layer):
    return pl.pallas_call(
        _proj_kernel,
        grid=(pl.cdiv(IN_COLS, PROJ_TN), N_TOK // PROJ_TM),
        in_specs=[pl.BlockSpec((PROJ_TM, D_MODEL), lambda j, i: (i, 0)),
                  pl.BlockSpec((None, PROJ_TN, D_MODEL), lambda j, i: (layer, j, 0))],
        out_specs=pl.BlockSpec((PROJ_TM, PROJ_TN), lambda j, i: (i, j)),
        out_shape=jax.ShapeDtypeStruct((N_TOK, IN_COLS), F32),
        scratch_shapes=[pltpu.VMEM((PROJ_TN, D_MODEL), BF16)],
        compiler_params=_params("arbitrary", "arbitrary", vmem_limit=VMEM_LIMIT_BIG),
        name="in_proj",
    )(xb, w_in_t)


def _head_column(values, rows_per_head, n_rows):
    grp = lax.broadcasted_iota(jnp.int32, (n_rows, 1), 0) // rows_per_head
    col = jnp.full((n_rows, 1), values[-1], F32)
    for g in range(len(values) - 2, -1, -1):
        col = jnp.where(grp == g, values[g], col)
    return col


def _softmax_with_sink(s, sink_col):
    m = jnp.maximum(jnp.max(s, axis=-1, keepdims=True), sink_col)
    p = jnp.exp(s - m)
    denom = jnp.sum(p, axis=-1, keepdims=True) + jnp.exp(sink_col - m)
    return p, denom


def _attn_prompt_kernel(sink_ref, q_ref, kc_ref, kp_ref, vc_ref, vp_ref, o_ref):
    blk = pl.program_id(1)
    n_rows = GQA * WINDOW
    qpos = lax.broadcasted_iota(jnp.int32, (n_rows, 2 * WINDOW), 0) % WINDOW
    kidx = lax.broadcasted_iota(jnp.int32, (n_rows, 2 * WINDOW), 1)
    dist_i = qpos + WINDOW - kidx
    valid = (dist_i >= 0) & (dist_i <= WINDOW) & ((kidx >= WINDOW) | (blk > 0))
    dist = dist_i.astype(F32)
    for kvh in range(KV_HEADS):
        cols = slice(kvh * HEAD_DIM, (kvh + 1) * HEAD_DIM)
        kk = jnp.concatenate([kp_ref[:, cols], kc_ref[:, cols]], axis=0)
        vv = jnp.concatenate([vp_ref[:, cols], vc_ref[:, cols]], axis=0)
        heads = [kvh * GQA + g for g in range(GQA)]
        qs = jnp.concatenate([q_ref[:, h * HEAD_DIM:(h + 1) * HEAD_DIM] for h in heads], axis=0)
        slope = _head_column([2.0 ** (-8.0 * (h + 1) / ATTN_HEADS) for h in heads], WINDOW, n_rows)
        sink = _head_column([sink_ref[h] for h in heads], WINDOW, n_rows)
        s = _mm_nt(qs, kk) * (HEAD_DIM ** -0.5) - slope * dist
        s = jnp.where(valid, s, NEG_INF)
        p, denom = _softmax_with_sink(s, sink)
        o = _mm(p, vv) / denom
        for g, h in enumerate(heads):
            o_ref[:, h * HEAD_DIM:(h + 1) * HEAD_DIM] = o[g * WINDOW:(g + 1) * WINDOW].astype(BF16)


def _attn_prompt(proj, sinks):
    nb = SEQ // WINDOW
    kcol, vcol = COL_K // KV_WIDTH, COL_V // KV_WIDTH

    def cur(c):
        return lambda b, i: (b * nb + i, c)

    def prev(c):
        return lambda b, i: (jnp.maximum(b * nb + i - 1, 0), c)

    return pl.pallas_call(
        _attn_prompt_kernel,
        grid=(BATCH, nb),
        in_specs=[pl.BlockSpec(memory_space=pltpu.SMEM),
                  pl.BlockSpec((WINDOW, ATTN_WIDTH), cur(0)),
                  pl.BlockSpec((WINDOW, KV_WIDTH), cur(kcol)),
                  pl.BlockSpec((WINDOW, KV_WIDTH), prev(kcol)),
                  pl.BlockSpec((WINDOW, KV_WIDTH), cur(vcol)),
                  pl.BlockSpec((WINDOW, KV_WIDTH), prev(vcol))],
        out_specs=pl.BlockSpec((WINDOW, ATTN_WIDTH), lambda b, i: (b * nb + i, 0)),
        out_shape=jax.ShapeDtypeStruct((N_PROMPT, ATTN_WIDTH), BF16),
        compiler_params=_params("arbitrary", "arbitrary"),
        name="attn_prompt",
    )(sinks, proj, proj, proj, proj, proj)


def _attn_sample_kernel(sink_ref, q_ref, kn_ref, vn_ref, kb_ref, vb_ref, o_ref, ko_ref, vo_ref):
    n_rows = GQA * DEC_SEQ
    tpos = lax.broadcasted_iota(jnp.int32, (n_rows, 2 * WINDOW), 0) % DEC_SEQ
    kidx = lax.broadcasted_iota(jnp.int32, (n_rows, 2 * WINDOW), 1)
    dist_i = jnp.where(kidx < WINDOW, tpos + WINDOW - kidx, tpos - (kidx - WINDOW))
    valid = (dist_i >= 0) & (dist_i <= WINDOW) & (kidx < WINDOW + DEC_SEQ)
    dist = dist_i.astype(F32)
    pad = jnp.zeros((WINDOW - DEC_SEQ, HEAD_DIM), F32)
    for bb in range(SAMPLE_BB):
        rows = slice(bb * DEC_SEQ, (bb + 1) * DEC_SEQ)
        for kvh in range(KV_HEADS):
            cols = slice(kvh * HEAD_DIM, (kvh + 1) * HEAD_DIM)
            kk = jnp.concatenate([kb_ref[bb, :, cols], kn_ref[rows, cols], pad], axis=0)
            vv = jnp.concatenate([vb_ref[bb, :, cols], vn_ref[rows, cols], pad], axis=0)
            heads = [kvh * GQA + g for g in range(GQA)]
            qs = jnp.concatenate([q_ref[rows, h * HEAD_DIM:(h + 1) * HEAD_DIM] for h in heads], axis=0)
            slope = _head_column([2.0 ** (-8.0 * (h + 1) / ATTN_HEADS) for h in heads], DEC_SEQ, n_rows)
            sink = _head_column([sink_ref[h] for h in heads], DEC_SEQ, n_rows)
            s = _mm_nt(qs, kk) * (HEAD_DIM ** -0.5) - slope * dist
            s = jnp.where(valid, s, NEG_INF)
            p, denom = _softmax_with_sink(s, sink)
            o = _mm(p, vv) / denom
            for g, h in enumerate(heads):
                o_ref[rows, h * HEAD_DIM:(h + 1) * HEAD_DIM] = o[g * DEC_SEQ:(g + 1) * DEC_SEQ]
        ko_ref[bb, 0:WINDOW - DEC_SEQ, :] = kb_ref[bb, DEC_SEQ:WINDOW, :]
        ko_ref[bb, WINDOW - DEC_SEQ:WINDOW, :] = kn_ref[rows, :]
        vo_ref[bb, 0:WINDOW - DEC_SEQ, :] = vb_ref[bb, DEC_SEQ:WINDOW, :]
        vo_ref[bb, WINDOW - DEC_SEQ:WINDOW, :] = vn_ref[rows, :]


def _attn_sample(proj, sinks, k_buf, v_buf, layer):
    rows = SAMPLE_BB * DEC_SEQ
    row0 = N_PROMPT // rows
    kcol, vcol = COL_K // KV_WIDTH, COL_V // KV_WIDTH
    cache_in = pl.BlockSpec((None, SAMPLE_BB, WINDOW, KV_WIDTH), lambda i: (layer, i, 0, 0))
    cache = pl.BlockSpec((SAMPLE_BB, WINDOW, KV_WIDTH), lambda i: (i, 0, 0))
    cache_shape = jax.ShapeDtypeStruct((DEC_BATCH, WINDOW, KV_WIDTH), F32)
    return pl.pallas_call(
        _attn_sample_kernel,
        grid=(DEC_BATCH // SAMPLE_BB,),
        in_specs=[pl.BlockSpec(memory_space=pltpu.SMEM),
                  pl.BlockSpec((rows, ATTN_WIDTH), lambda i: (row0 + i, 0)),
                  pl.BlockSpec((rows, KV_WIDTH), lambda i: (row0 + i, kcol)),
                  pl.BlockSpec((rows, KV_WIDTH), lambda i: (row0 + i, vcol)),
                  cache_in, cache_in],
        out_specs=[pl.BlockSpec((rows, ATTN_WIDTH), lambda i: (i, 0)), cache, cache],
        out_shape=[jax.ShapeDtypeStruct((N_SAMPLE, ATTN_WIDTH), F32), cache_shape, cache_shape],
        compiler_params=_params("arbitrary"),
        name="attn_sample",
    )(sinks, proj, proj, proj, k_buf, v_buf)


def _cumsum_rows(x):
    n = x.shape[0]
    row = lax.broadcasted_iota(jnp.int32, x.shape, 0)
    step = 1
    while step < n:
        x = x + jnp.where(row >= step, pltpu.roll(x, step, axis=0), 0.0)
        step *= 2
    return x


def _gdn_kernel(*refs, chunk, rows_in, has_state):
    if has_state:
        b1_ref, b2_ref, b3_ref, cw_ref, gp_ref, nw_ref, st_ref, s0_ref, o_ref, s_ref, ext_ref = refs
        first = True
    else:
        b1_ref, b2_ref, b3_ref, cw_ref, gp_ref, nw_ref, o_ref, s_ref, ext_ref = refs
        first = pl.program_id(1) == 0

    if has_state:
        ext_ref[0:8, :] = st_ref[...]
        s_ref[...] = s0_ref[...]
    else:
        @pl.when(first)
        def _():
            ext_ref[0:8, :] = jnp.zeros((8, CONV_CH), F32)
            s_ref[...] = jnp.zeros_like(s_ref)

        @pl.when(jnp.logical_not(first))
        def _():
            ext_ref[0:8, :] = ext_ref[chunk:chunk + 8, :]

    ext_ref[8:8 + rows_in, 0:GDN_COLS] = b1_ref[...]
    ext_ref[8:8 + rows_in, GDN_COLS:CONV_CH] = b2_ref[...]
    if rows_in < chunk:
        ext_ref[8 + rows_in:8 + chunk, :] = jnp.zeros((chunk - rows_in, CONV_CH), F32)

    lane = lax.broadcasted_iota(jnp.int32, (chunk, LANES), 1)
    row = lax.broadcasted_iota(jnp.int32, (chunk, LANES), 0)
    if rows_in < chunk:
        ab = jnp.concatenate([b3_ref[:, DN_VW:DN_VW + LANES],
                              jnp.zeros((chunk - rows_in, LANES), F32)], axis=0)
    else:
        ab = b3_ref[:, DN_VW:DN_VW + LANES]
    live = (lane < 2 * DN_HEADS) & (row < rows_in)
    ab = jnp.where(live, ab, 0.0)
    a_scale = gp_ref[0:1, :]
    dt_bias = gp_ref[1:2, :]
    g_all = jnp.where(live, a_scale * jax.nn.softplus(ab + dt_bias), 0.0)
    beta_all = jnp.where(live, jax.nn.sigmoid(ab), 0.0)
    gcum = _cumsum_rows(g_all)
    gcum_t = gcum.T
    g_last_all = gcum[chunk - 1:chunk, :]

    ri = lax.broadcasted_iota(jnp.int32, (chunk, chunk), 0)
    ci = lax.broadcasted_iota(jnp.int32, (chunk, chunk), 1)
    incl = ri >= ci
    strict = ri > ci

    def conv_strip(col0):
        cols = slice(col0, col0 + LANES)
        acc = ext_ref[8:8 + chunk, cols] * cw_ref[CONV_W - 1:CONV_W, cols]
        for tap in range(1, CONV_W):
            acc = acc + ext_ref[8 - tap:8 - tap + chunk, cols] * cw_ref[CONV_W - 1 - tap:CONV_W - tap, cols]
        return _silu(acc)

    def l2norm(x):
        return x * lax.rsqrt(jnp.sum(x * x, axis=-1, keepdims=True) + RMS_EPS)

    heads = range(DN_HEADS)
    q = [l2norm(conv_strip(h * DN_DK)) * (DN_DK ** -0.5) for h in heads]
    k = [l2norm(conv_strip(DN_KW + h * DN_DK)) for h in heads]
    v = [conv_strip(2 * DN_KW + h * DN_DV) for h in heads]
    gc_col = [gcum[:, h:h + 1] for h in heads]
    g_last = [g_last_all[:, h:h + 1] for h in heads]
    beta = [beta_all[:, DN_HEADS + h:DN_HEADS + h + 1] for h in heads]
    decay = [jnp.where(incl, jnp.exp(jnp.where(incl, gc_col[h] - gcum_t[h:h + 1, :], 0.0)), 0.0) for h in heads]
    eg = [jnp.exp(gc_col[h]) for h in heads]
    kb = [k[h] * beta[h] for h in heads]
    kk = [_mm_nt(jnp.concatenate([kb[h], q[h]], axis=0), k[h]) for h in heads]
    qk = [jnp.where(incl, kk[h][chunk:] * decay[h], 0.0) for h in heads]
    neg_a = [jnp.where(strict, -kk[h][:chunk] * decay[h], 0.0) for h in heads]
    base = min(GDN_INV_BASE, chunk)
    p = [jnp.where(ri // base == ci // base, neg_a[h], 0.0) for h in heads]
    eye = (ri == ci).astype(F32)
    inv = [eye + p[h] for h in heads]
    span = 2
    while span < base:
        p = [_mm(p[h], p[h]) for h in heads]
        inv = [inv[h] + _mm(inv[h], p[h]) for h in heads]
        span *= 2
    size = base
    while size < chunk:
        lower_left = (ri // (2 * size) == ci // (2 * size)) & (ri // size != ci // size)
        off = [jnp.where(lower_left, neg_a[h], 0.0) for h in heads]
        inv = [inv[h] + _mm(_mm(inv[h], off[h]), inv[h]) for h in heads]
        size *= 2
    sol = [_mm(inv[h], jnp.concatenate([v[h] * beta[h], kb[h] * eg[h]], axis=1)) for h in heads]
    s_prev = [s_ref[h] for h in heads]
    ws = [_mm(jnp.concatenate([sol[h][:, DN_DV:], q[h] * eg[h]], axis=0), s_prev[h]) for h in heads]
    v_new = [sol[h][:, :DN_DV] - ws[h][:chunk] for h in heads]
    o = [ws[h][chunk:] + _mm(qk[h], v_new[h]) for h in heads]
    k_dec = [k[h] * jnp.exp(g_last[h] - gc_col[h]) for h in heads]
    for h in heads:
        s_ref[h] = s_prev[h] * jnp.exp(g_last[h]) + _mm_tn(k_dec[h], v_new[h])
    for h in heads:
        on = o[h] * lax.rsqrt(jnp.mean(o[h] * o[h], axis=-1, keepdims=True) + RMS_EPS)
        z = b3_ref[:, h * DN_DV:(h + 1) * DN_DV]
        out = on[0:rows_in] * nw_ref[...] * _silu(z)
        o_ref[:, h * DN_DV:(h + 1) * DN_DV] = out.astype(o_ref.dtype)


def _gdn_gate_params(a_log, dt_bias):
    rows = jnp.stack([-jnp.exp(a_log.astype(F32)), dt_bias.astype(F32)], axis=1)
    return jnp.pad(rows, ((0, 0), (0, 0), (0, LANES - DN_HEADS)))


def _gdn_prompt(proj, conv_w, gp, norm_w, layer):
    nc = SEQ // GDN_CHUNK
    c0 = COL_QKV // GDN_COLS

    def win(c):
        return pl.BlockSpec((GDN_CHUNK, GDN_COLS), lambda b, i: (b * nc + i, c))

    return pl.pallas_call(
        functools.partial(_gdn_kernel, chunk=GDN_CHUNK, rows_in=GDN_CHUNK, has_state=False),
        grid=(BATCH, nc),
        in_specs=[win(c0), win(c0 + 1), win(c0 + 2), _layer_slab((CONV_W, CONV_CH), layer),
                  _layer_slab((2, LANES), layer), _layer_slab((1, DN_DV), layer)],
        out_specs=[pl.BlockSpec((GDN_CHUNK, DN_VW), lambda b, i: (b * nc + i, 0)),
                   pl.BlockSpec((None, DN_HEADS, DN_DK, DN_DV), lambda b, i: (b, 0, 0, 0))],
        out_shape=[jax.ShapeDtypeStruct((N_PROMPT, DN_VW), BF16),
                   jax.ShapeDtypeStruct((BATCH, DN_HEADS, DN_DK, DN_DV), F32)],
        scratch_shapes=[pltpu.VMEM((8 + GDN_CHUNK, CONV_CH), F32)],
        compiler_params=_params("arbitrary", "arbitrary"),
        name="gdn_prompt",
    )(proj, proj, proj, conv_w, gp, norm_w)


def _gdn_sample(proj, conv_w, gp, norm_w, conv_state8, s0, chunk, layer):
    row0 = N_PROMPT // DEC_SEQ
    c0 = COL_QKV // GDN_COLS

    def win(c):
        return pl.BlockSpec((DEC_SEQ, GDN_COLS), lambda b: (row0 + b, c))

    state = pl.BlockSpec((None, DN_HEADS, DN_DK, DN_DV), lambda b: (b, 0, 0, 0))
    return pl.pallas_call(
        functools.partial(_gdn_kernel, chunk=chunk, rows_in=DEC_SEQ, has_state=True),
        grid=(DEC_BATCH,),
        in_specs=[win(c0), win(c0 + 1), win(c0 + 2), _layer_slab((CONV_W, CONV_CH), layer),
                  _layer_slab((2, LANES), layer), _layer_slab((1, DN_DV), layer),
                  pl.BlockSpec((None, None, 8, CONV_CH), lambda b: (layer, b, 0, 0)),
                  pl.BlockSpec((None, None, DN_HEADS, DN_DK, DN_DV), lambda b: (layer, b, 0, 0, 0))],
        out_specs=[pl.BlockSpec((DEC_SEQ, DN_VW), lambda b: (b, 0)), state],
        out_shape=[jax.ShapeDtypeStruct((N_SAMPLE, DN_VW), F32),
                   jax.ShapeDtypeStruct((DEC_BATCH, DN_HEADS, DN_DK, DN_DV), F32)],
        scratch_shapes=[pltpu.VMEM((8 + chunk, CONV_CH), F32)],
        compiler_params=_params("arbitrary"),
        name="gdn_sample",
    )(proj, proj, proj, conv_w, gp, norm_w, conv_state8, s0)


def _out_router_kernel(map_ref, mas_ref, mdp_ref, mds_ref, x_ref, wo_ref, g_ref, b_ref, wr_ref, br_ref,
                       x1_ref, eid_ref, gate_ref):
    def emit(ma_ref, md_ref):
        mix = (jnp.dot(ma_ref[...].astype(BF16), wo_ref[0:ATTN_WIDTH, :], preferred_element_type=F32)
               + jnp.dot(md_ref[...].astype(BF16), wo_ref[ATTN_WIDTH:MIX_WIDTH, :], preferred_element_type=F32))
        x1 = _layer_norm(DEEPNORM_ALPHA * x_ref[...] + mix, g_ref[...], b_ref[...])
        x1_ref[...] = x1
        logits = jnp.dot(x1.astype(BF16), wr_ref[...], preferred_element_type=F32) + br_ref[...]
        eid, gate = _route(logits)
        eid_ref[...] = eid
        gate_ref[...] = gate

    is_prompt = pl.program_id(0) < N_PROMPT // ROW_TILE
    pl.when(is_prompt)(lambda: emit(map_ref, mdp_ref))
    pl.when(jnp.logical_not(is_prompt))(lambda: emit(mas_ref, mds_ref))


def _route(logits):
    lane = lax.broadcasted_iota(jnp.int32, logits.shape, 1)
    lane_f = lane.astype(F32)
    far = float(LANES)
    is_grp = lane < N_GROUPS
    grp_max = jnp.max(jnp.where(is_grp, logits, NEG_INF), axis=-1, keepdims=True)
    grp = jnp.min(jnp.where(is_grp & (logits == grp_max), lane_f, far), axis=-1, keepdims=True)
    grp_gate = 1.0 / jnp.sum(jnp.where(is_grp, jnp.exp(logits - grp_max), 0.0), axis=-1, keepdims=True)
    lo = N_GROUPS + EXPERTS_PER_GROUP * grp
    in_grp = (lane_f >= lo) & (lane_f < lo + EXPERTS_PER_GROUP)
    v1 = jnp.max(jnp.where(in_grp, logits, NEG_INF), axis=-1, keepdims=True)
    i1 = jnp.min(jnp.where(in_grp & (logits == v1), lane_f, far), axis=-1, keepdims=True)
    rest = in_grp & (lane_f != i1)
    v2 = jnp.max(jnp.where(rest, logits, NEG_INF), axis=-1, keepdims=True)
    i2 = jnp.min(jnp.where(rest & (logits == v2), lane_f, far), axis=-1, keepdims=True)
    t = jnp.exp(v2 - v1)
    g1 = grp_gate / (1.0 + t)
    g2 = g1 * t
    e1 = (i1 - N_GROUPS).astype(jnp.int32)
    e2 = (i2 - N_GROUPS).astype(jnp.int32)
    return (jnp.where(lane == 0, e1, jnp.where(lane == 1, e2, 0)),
            jnp.where(lane == 0, g1, jnp.where(lane == 1, g2, 0.0)))


def _out_router(mix_a_p, mix_a_s, mix_d_p, mix_d_s, x, wo_b, g, b, wr_b, br, layer):
    assert N_SAMPLE == ROW_TILE

    def row(w):
        return pl.BlockSpec((ROW_TILE, w), lambda i: (i, 0))

    last_prompt = N_PROMPT // ROW_TILE - 1
    prompt = pl.BlockSpec((ROW_TILE, ATTN_WIDTH), lambda i: (jnp.minimum(i, last_prompt), 0))
    sample = pl.BlockSpec((ROW_TILE, ATTN_WIDTH), lambda i: (0, 0))
    return pl.pallas_call(
        _out_router_kernel,
        grid=(N_TOK // ROW_TILE,),
        in_specs=[prompt, sample, prompt, sample, row(D_MODEL), _layer_slab((MIX_WIDTH, D_MODEL), layer),
                  _layer_slab((1, D_MODEL), layer), _layer_slab((1, D_MODEL), layer),
                  _layer_slab((D_MODEL, LANES), layer), _layer_slab((1, LANES), layer)],
        out_specs=[row(D_MODEL), row(LANES), row(LANES)],
        out_shape=[jax.ShapeDtypeStruct((N_TOK, D_MODEL), F32),
                   jax.ShapeDtypeStruct((N_TOK, LANES), jnp.int32),
                   jax.ShapeDtypeStruct((N_TOK, LANES), F32)],
        compiler_params=_params("arbitrary"),
        name="out_router",
    )(mix_a_p, mix_a_s, mix_d_p, mix_d_s, x, wo_b, g, b, wr_b, br)


def _plan_kernel(eid_ref, pos_ref, cnt_ref, run_ref, off_ref):
    phase = pl.program_id(0)
    i = pl.program_id(1)
    lane = lax.broadcasted_iota(jnp.int32, (ROW_TILE, LANES), 1)
    eid = eid_ref[...]
    hot0 = (lane == eid[:, 0:1]).astype(F32)
    hot1 = (lane == eid[:, 1:2]).astype(F32)
    hot = hot0 + hot1
    tile_cnt = jnp.sum(hot, axis=0, keepdims=True)

    @pl.when((phase == 0) & (i == 0))
    def _():
        cnt_ref[...] = jnp.zeros_like(cnt_ref)

    @pl.when(phase == 0)
    def _():
        cnt_ref[...] += tile_cnt

    @pl.when((phase == 1) & (i == 0))
    def _():
        cnt = cnt_ref[...]
        hi = jnp.floor(cnt * (1.0 / 256.0))
        lo = cnt - 256.0 * hi
        r = lax.broadcasted_iota(jnp.int32, (LANES, LANES), 0)
        c = lax.broadcasted_iota(jnp.int32, (LANES, LANES), 1)
        upper = (r < c).astype(F32)
        hi8 = jnp.broadcast_to(hi, (8, LANES))
        lo8 = jnp.broadcast_to(lo, (8, LANES))
        off = 256.0 * _mm(hi8, upper) + _mm(lo8, upper)
        off_ref[...] = off[0:1]
        run_ref[...] = jnp.zeros_like(run_ref)

    @pl.when(phase == 1)
    def _():
        r = lax.broadcasted_iota(jnp.int32, (ROW_TILE, ROW_TILE), 0)
        c = lax.broadcasted_iota(jnp.int32, (ROW_TILE, ROW_TILE), 1)
        before = (r > c).astype(F32)
        slot = _mm(before, hot) + run_ref[...] + off_ref[...]
        p0 = jnp.sum(hot0 * slot, axis=-1, keepdims=True).astype(jnp.int32)
        p1 = jnp.sum(hot1 * slot, axis=-1, keepdims=True).astype(jnp.int32)
        pos_ref[...] = jnp.where(lane == 0, p0, jnp.where(lane == 1, p1, 0))
        run_ref[...] += tile_cnt


def _plan(eid):
    return pl.pallas_call(
        _plan_kernel,
        grid=(2, N_TOK // ROW_TILE),
        in_specs=[pl.BlockSpec((ROW_TILE, LANES), lambda p, i: (i, 0))],
        out_specs=[pl.BlockSpec((ROW_TILE, LANES), lambda p, i: (i * p, 0)),
                   pl.BlockSpec((1, LANES), lambda p, i: (0, 0))],
        out_shape=[jax.ShapeDtypeStruct((N_TOK, LANES), jnp.int32),
                   jax.ShapeDtypeStruct((1, LANES), F32)],
        scratch_shapes=[pltpu.VMEM((1, LANES), F32), pltpu.VMEM((1, LANES), F32)],
        compiler_params=_params("arbitrary", "arbitrary"),
        name="moe_plan",
    )(eid)


def _work_items(counts):
    counts = counts.astype(jnp.int32)
    ends = jnp.cumsum(counts)
    starts = ends - counts
    first_tile = starts // MOE_TM
    n_tiles = jnp.where(counts > 0, (ends - 1) // MOE_TM - first_tile + 1, 0)
    item_end = jnp.cumsum(n_tiles)
    item_start = item_end - n_tiles
    n_items = item_end[-1]
    w = jnp.minimum(jnp.arange(N_MOE_ITEMS, dtype=jnp.int32), n_items - 1)
    expert = jnp.sum((item_end[None, :] <= w[:, None]).astype(jnp.int32), axis=1)
    tile = first_tile[expert] + (w - item_start[expert])
    prev_tile = jnp.concatenate([jnp.full((1,), -1, jnp.int32), tile[:-1]])
    prev_expert = jnp.concatenate([jnp.full((1,), -1, jnp.int32), expert[:-1]])
    valid = (jnp.arange(N_MOE_ITEMS, dtype=jnp.int32) < n_items).astype(jnp.int32)
    return (tile, expert, (tile != prev_tile).astype(jnp.int32), (expert != prev_expert).astype(jnp.int32),
            valid, starts[expert], ends[expert])


def _slot_tokens_kernel(pos_ref, tok_ref):
    def place(a, carry):
        tok_ref[pos_ref[a]] = a // 2
        return carry

    lax.fori_loop(0, N_ASSIGN, place, 0, unroll=8)


def _slot_tokens(pos_flat):
    return pl.pallas_call(
        _slot_tokens_kernel,
        grid_spec=pltpu.PrefetchScalarGridSpec(
            num_scalar_prefetch=1,
            grid=(1,),
            in_specs=[],
            out_specs=pl.BlockSpec(memory_space=pltpu.SMEM)),
        out_shape=jax.ShapeDtypeStruct((N_ASSIGN,), jnp.int32),
        compiler_params=_params("arbitrary"),
        name="moe_slot_tokens",
    )(pos_flat)


def _experts_kernel(tile_ref, exp_ref, first_ref, newexp_ref, valid_ref, gstart_ref, gend_ref, tok_ref,
                    x1_ref, wgu_ref, wdn_ref, o_ref, wgu_b, wdn_b, xs_ref, sem):
    w = pl.program_id(0)

    def start_gather(tile, into):
        base = tile * MOE_TM

        def issue(r, carry):
            pltpu.make_async_copy(x1_ref.at[pl.ds(tok_ref[base + r], 1), :],
                                  xs_ref.at[into, pl.ds(r, 1), :], sem.at[into]).start()
            return carry

        lax.fori_loop(0, MOE_TM, issue, 0, unroll=8)

    @pl.when(w == 0)
    def _():
        start_gather(0, 0)

    @pl.when(valid_ref[w] == 1)
    def _():
        tile = tile_ref[w]
        buf = tile % 2

        @pl.when(first_ref[w] == 1)
        def _():
            pltpu.make_async_copy(x1_ref.at[pl.ds(0, MOE_TM), :], xs_ref.at[buf], sem.at[buf]).wait()

            @pl.when(tile + 1 < N_MOE_TILES)
            def _():
                start_gather(tile + 1, 1 - buf)

        @pl.when(newexp_ref[w] == 1)
        def _():
            wgu_b[...] = wgu_ref[...].astype(BF16)
            wdn_b[...] = wdn_ref[...].astype(BF16)

        h = jnp.dot(xs_ref[buf].astype(BF16), wgu_b[...], preferred_element_type=F32)
        act = _silu(h[:, :D_FF_EXPERT]) * h[:, D_FF_EXPERT:]
        y = jnp.dot(act.astype(BF16), wdn_b[...], preferred_element_type=F32)
        rows = tile_ref[w] * MOE_TM + lax.broadcasted_iota(jnp.int32, (MOE_TM, 1), 0)
        mine = (rows >= gstart_ref[w]) & (rows < gend_ref[w])

        @pl.when(first_ref[w] == 1)
        def _():
            o_ref[...] = jnp.where(mine, y, 0.0)

        @pl.when(first_ref[w] == 0)
        def _():
            o_ref[...] = jnp.where(mine, y, o_ref[...])


def _experts(items, slot_tokens, x1, w_gu, w_dn, layer):
    return pl.pallas_call(
        _experts_kernel,
        grid_spec=pltpu.PrefetchScalarGridSpec(
            num_scalar_prefetch=8,
            grid=(N_MOE_ITEMS,),
            in_specs=[pl.BlockSpec(memory_space=pl.ANY),
                      pl.BlockSpec((None, None, D_MODEL, 2 * D_FF_EXPERT),
                                   lambda w, t, e, *_: (layer, e[w], 0, 0)),
                      pl.BlockSpec((None, None, D_FF_EXPERT, D_MODEL),
                                   lambda w, t, e, *_: (layer, e[w], 0, 0))],
            out_specs=pl.BlockSpec((MOE_TM, D_MODEL), lambda w, t, e, *_: (t[w], 0)),
            scratch_shapes=[pltpu.VMEM((D_MODEL, 2 * D_FF_EXPERT), BF16),
                            pltpu.VMEM((D_FF_EXPERT, D_MODEL), BF16),
                            pltpu.VMEM((2, MOE_TM, D_MODEL), F32),
                            pltpu.SemaphoreType.DMA((2,))]),
        out_shape=jax.ShapeDtypeStruct((N_ASSIGN, D_MODEL), F32),
        compiler_params=_params("arbitrary"),
        name="moe_experts",
    )(*items, slot_tokens, x1, w_gu, w_dn)


def _combine_kernel(pos_ref, ys_ref, x1_ref, gate_ref, g_ref, b_ref, out_a_ref, out_b_ref, rows_ref, sem, *,
                    last_layer):
    i = pl.program_id(0)
    buf = i % 2

    def start_gather(tile, into):
        base = tile * (2 * ROW_TILE)

        def issue(r, carry):
            for pick in range(2):
                pltpu.make_async_copy(ys_ref.at[pl.ds(pos_ref[base + 2 * r + pick], 1), :],
                                      rows_ref.at[into, pick, pl.ds(r, 1), :], sem.at[into]).start()
            return carry

        lax.fori_loop(0, ROW_TILE, issue, 0, unroll=8)

    @pl.when(i == 0)
    def _():
        start_gather(0, 0)

    @pl.when(i + 1 < pl.num_programs(0))
    def _():
        start_gather(i + 1, 1 - buf)

    for pick in range(2):
        pltpu.make_async_copy(ys_ref.at[pl.ds(0, ROW_TILE), :], rows_ref.at[buf, pick], sem.at[buf]).wait()
    gate = gate_ref[...]
    y = gate[:, 0:1] * rows_ref[buf, 0] + gate[:, 1:2] * rows_ref[buf, 1]
    x2 = _layer_norm(DEEPNORM_ALPHA * x1_ref[...] + y, g_ref[...], b_ref[...])
    if last_layer:
        is_prompt = i < N_PROMPT // ROW_TILE

        @pl.when(is_prompt)
        def _():
            out_a_ref[...] = x2

        @pl.when(jnp.logical_not(is_prompt))
        def _():
            out_b_ref[...] = x2
    else:
        out_a_ref[...] = x2
        out_b_ref[...] = x2.astype(BF16)


def _combine(pos_flat, ys, x1, gates, g, b, layer):
    row = lambda w: pl.BlockSpec((ROW_TILE, w), lambda i, pos: (i, 0))
    vec = _layer_slab((1, D_MODEL), layer)
    last_layer = layer == DEPTH - 1
    if last_layer:
        assert N_SAMPLE == ROW_TILE
        last_prompt = N_PROMPT // ROW_TILE - 1
        out_specs = [pl.BlockSpec((ROW_TILE, D_MODEL), lambda i, pos: (jnp.minimum(i, last_prompt), 0)),
                     pl.BlockSpec((ROW_TILE, D_MODEL), lambda i, pos: (0, 0))]
        out_shape = [jax.ShapeDtypeStruct((N_PROMPT, D_MODEL), F32), jax.ShapeDtypeStruct((N_SAMPLE, D_MODEL), F32)]
    else:
        out_specs = [row(D_MODEL), row(D_MODEL)]
        out_shape = [jax.ShapeDtypeStruct((N_TOK, D_MODEL), F32), jax.ShapeDtypeStruct((N_TOK, D_MODEL), BF16)]
    return pl.pallas_call(
        functools.partial(_combine_kernel, last_layer=last_layer),
        grid_spec=pltpu.PrefetchScalarGridSpec(
            num_scalar_prefetch=1,
            grid=(N_TOK // ROW_TILE,),
            in_specs=[pl.BlockSpec(memory_space=pl.ANY), row(D_MODEL), row(LANES), vec, vec],
            out_specs=out_specs,
            scratch_shapes=[pltpu.VMEM((2, 2, ROW_TILE, D_MODEL), F32),
                            pltpu.SemaphoreType.DMA((2,))]),
        out_shape=out_shape,
        compiler_params=_params("arbitrary"),
        name="moe_combine",
    )(pos_flat, ys, x1, gates, g, b)


def _router_weights(wg, bg, we, be):
    pad = LANES - N_GROUPS - N_EXPERTS
    wr = jnp.concatenate([wg, we, jnp.zeros((DEPTH, D_MODEL, pad), wg.dtype)], axis=-1)
    br = jnp.concatenate([bg, be, jnp.zeros((DEPTH, pad), bg.dtype)], axis=-1)
    return wr.astype(BF16), br.astype(F32).reshape(DEPTH, 1, LANES)


def kernel(x_prompt, x_sample, cache_swa_k, cache_swa_v, state_conv, state_delta, ln_in_g, ln_in_b, w_in, conv_w,
           a_log, dt_bias, dn_norm_w, attn_sinks, w_out, ln1_g, ln1_b, router_group_w, router_group_b,
           router_expert_w, router_expert_b, w_gate_up, w_down, ln2_g, ln2_b):
    x, xb = _ln_in(x_prompt.reshape(N_PROMPT, D_MODEL), x_sample.reshape(N_SAMPLE, D_MODEL), ln_in_g, ln_in_b)
    k_buf = cache_swa_k.reshape(DEPTH, DEC_BATCH, WINDOW, KV_WIDTH)
    v_buf = cache_swa_v.reshape(DEPTH, DEC_BATCH, WINDOW, KV_WIDTH)
    conv_state8 = jnp.pad(state_conv, ((0, 0), (0, 0), (8 - (CONV_W - 1), 0), (0, 0)))
    gp = _gdn_gate_params(a_log, dt_bias)
    norm_w = dn_norm_w.reshape(DEPTH, 1, DN_DV).astype(F32)
    wo_b = w_out.astype(BF16)
    w_in_t = jnp.swapaxes(w_in, 1, 2)
    wr_b, br = _router_weights(router_group_w, router_group_b, router_expert_w, router_expert_b)
    ln1_g, ln1_b, ln2_g, ln2_b = (a.reshape(DEPTH, 1, D_MODEL) for a in (ln1_g, ln1_b, ln2_g, ln2_b))
    kp, vp, cp, sp = [], [], [], []
    kq, vq, cq, sq = [], [], [], []
    for l in range(DEPTH):
        proj = _proj(xb, w_in_t, l)
        sinks = attn_sinks[l].astype(F32)
        o_attn_p = _attn_prompt(proj, sinks)
        o_attn_s, k_s, v_s = _attn_sample(proj, sinks, k_buf, v_buf, l)
        o_dn_p, s_p = _gdn_prompt(proj, conv_w, gp, norm_w, l)
        o_dn_s, s_s = _gdn_sample(proj, conv_w, gp, norm_w, conv_state8, state_delta, GDN_CHUNK_SAMPLE, l)
        x1, eid, gates = _out_router(o_attn_p, o_attn_s, o_dn_p, o_dn_s, x, wo_b, ln1_g, ln1_b, wr_b, br, l)
        pos, counts = _plan(eid)
        pos_flat = pos[:, :2].reshape(N_ASSIGN)
        items = _work_items(counts[0, :N_EXPERTS])
        ys = _experts(items, _slot_tokens(pos_flat), x1, w_gate_up, w_down, l)
        x, xb = _combine(pos_flat, ys, x1, gates, ln2_g, ln2_b, l)

        def prompt_tail(rows, c0, c1):
            return jnp.stack([proj[(b + 1) * SEQ - rows:(b + 1) * SEQ, c0:c1] for b in range(BATCH)])

        kp.append(prompt_tail(WINDOW, COL_K, COL_V).reshape(BATCH, WINDOW, KV_HEADS, HEAD_DIM))
        vp.append(prompt_tail(WINDOW, COL_V, COL_QKV).reshape(BATCH, WINDOW, KV_HEADS, HEAD_DIM))
        cp.append(prompt_tail(CONV_W - 1, COL_QKV, COL_Z))
        sp.append(s_p)
        kq.append(k_s.reshape(DEC_BATCH, WINDOW, KV_HEADS, HEAD_DIM))
        vq.append(v_s.reshape(DEC_BATCH, WINDOW, KV_HEADS, HEAD_DIM))
        cq.append(proj[N_PROMPT:, COL_QKV:COL_Z].reshape(DEC_BATCH, DEC_SEQ, CONV_CH)[:, DEC_SEQ - (CONV_W - 1):])
        sq.append(s_s)
    return (x.reshape(BATCH, SEQ, D_MODEL), xb.reshape(DEC_BATCH, DEC_SEQ, D_MODEL),
            jnp.stack(kp), jnp.stack(vp), jnp.stack(cp), jnp.stack(sp),
            jnp.stack(kq), jnp.stack(vq), jnp.stack(cq), jnp.stack(sq))
```

```python
import functools

import jax
import jax.numpy as jnp
from jax import lax
from jax.experimental import pallas as pl
from jax.experimental.pallas import tpu as pltpu

D_MODEL = 2048
BATCH = 4
SEQ = 2048
DEPTH = 4
DEC_BATCH = 32
DEC_SEQ = 8
HEAD_DIM = 128
ATTN_HEADS = 8
KV_HEADS = 2
GQA = ATTN_HEADS // KV_HEADS
ATTN_WIDTH = ATTN_HEADS * HEAD_DIM
KV_WIDTH = KV_HEADS * HEAD_DIM
WINDOW = 128
DN_HEADS = 8
DN_DK = 128
DN_DV = 128
DN_KW = DN_HEADS * DN_DK
DN_VW = DN_HEADS * DN_DV
CONV_W = 4
CONV_CH = 2 * DN_KW + DN_VW
MIX_WIDTH = ATTN_WIDTH + DN_VW
IN_COLS = ATTN_WIDTH + 2 * KV_WIDTH + CONV_CH + DN_VW + 2 * DN_HEADS
N_GROUPS = 4
EXPERTS_PER_GROUP = 8
N_EXPERTS = N_GROUPS * EXPERTS_PER_GROUP
D_FF_EXPERT = 256
DEEPNORM_ALPHA = (2 * DEPTH) ** 0.25
LN_EPS = 1e-5
RMS_EPS = 1e-6
NEG_INF = -1e30

N_PROMPT = BATCH * SEQ
N_SAMPLE = DEC_BATCH * DEC_SEQ
N_TOK = N_PROMPT + N_SAMPLE
N_ASSIGN = 2 * N_TOK

COL_K = ATTN_WIDTH
COL_V = COL_K + KV_WIDTH
COL_QKV = COL_V + KV_WIDTH
COL_Z = COL_QKV + CONV_CH
COL_AB = COL_Z + DN_VW

LANES = 128
ROW_TILE = 256
PROJ_TM = 1408
PROJ_TN = 1024
GDN_CHUNK = 128
GDN_CHUNK_SAMPLE = 16
GDN_INV_BASE = 16
GDN_COLS = 1536
SAMPLE_BB = 4
ATTN_QB = 4
MOE_TM = 256
N_MOE_TILES = N_ASSIGN // MOE_TM
N_MOE_ITEMS = N_MOE_TILES + N_EXPERTS - 1
VMEM_LIMIT = 48 * 1024 * 1024
VMEM_LIMIT_BIG = 56 * 1024 * 1024

F32 = jnp.float32
BF16 = jnp.bfloat16


def _params(*sem, vmem_limit=VMEM_LIMIT):
    return pltpu.CompilerParams(dimension_semantics=sem, vmem_limit_bytes=vmem_limit)


def _mm(a, b):
    return jnp.dot(a.astype(BF16), b.astype(BF16), preferred_element_type=F32)


def _mm_nt(a, b):
    return lax.dot_general(a.astype(BF16), b.astype(BF16), (((1,), (1,)), ((), ())),
                           preferred_element_type=F32)


def _mm_tn(a, b):
    return lax.dot_general(a.astype(BF16), b.astype(BF16), (((0,), (0,)), ((), ())),
                           preferred_element_type=F32)


def _layer_slab(shape, layer):
    return pl.BlockSpec((None,) + tuple(shape), lambda *_: (layer,) + (0,) * len(shape),
                        pipeline_mode=pl.Buffered(1))


def _layer_norm(x, g, b):
    mu = jnp.mean(x, axis=-1, keepdims=True)
    xc = x - mu
    var = jnp.mean(xc * xc, axis=-1, keepdims=True)
    return xc * lax.rsqrt(var + LN_EPS) * g + b


def _silu(x):
    return x / (1.0 + jnp.exp(-x))


def _ln_in_kernel(xp_ref, xs_ref, g_ref, b_ref, o_ref, ob_ref):
    def emit(x_ref):
        y = _layer_norm(x_ref[...], g_ref[...], b_ref[...])
        o_ref[...] = y
        ob_ref[...] = y.astype(BF16)

    is_prompt = pl.program_id(0) < N_PROMPT // ROW_TILE
    pl.when(is_prompt)(lambda: emit(xp_ref))
    pl.when(jnp.logical_not(is_prompt))(lambda: emit(xs_ref))


def _ln_in(xp, xs, g, b):
    assert N_SAMPLE == ROW_TILE
    row = pl.BlockSpec((ROW_TILE, D_MODEL), lambda i: (i, 0))
    vec = pl.BlockSpec((1, D_MODEL), lambda i: (0, 0))
    last_prompt = N_PROMPT // ROW_TILE - 1
    return pl.pallas_call(
        _ln_in_kernel,
        grid=(N_TOK // ROW_TILE,),
        in_specs=[pl.BlockSpec((ROW_TILE, D_MODEL), lambda i: (jnp.minimum(i, last_prompt), 0)),
                  pl.BlockSpec((ROW_TILE, D_MODEL), lambda i: (0, 0)), vec, vec],
        out_specs=[row, row],
        out_shape=[jax.ShapeDtypeStruct((N_TOK, D_MODEL), F32),
                   jax.ShapeDtypeStruct((N_TOK, D_MODEL), BF16)],
        compiler_params=_params("arbitrary"),
        name="ln_in",
    )(xp, xs, g.reshape(1, D_MODEL), b.reshape(1, D_MODEL))


def _proj_kernel(x_ref, wt_ref, o_ref, wb_ref):
    @pl.when(pl.program_id(1) == 0)
    def _():
        wb_ref[...] = wt_ref[...].astype(BF16)

    o_ref[...] = lax.dot_general(x_ref[...], wb_ref[...], (((1,), (1,)), ((), ())), preferred_element_type=F32)


def _proj(xb, w_in_t, layer):
    return pl.pallas_call(
        _proj_kernel,
        grid=(pl.cdiv(IN_COLS, PROJ_TN), N_TOK // PROJ_TM),
        in_specs=[pl.BlockSpec((PROJ_TM, D_MODEL), lambda j, i: (i, 0)),
                  pl.BlockSpec((None, PROJ_TN, D_MODEL), lambda j, i: (layer, j, 0))],
        out_specs=pl.BlockSpec((PROJ_TM, PROJ_TN), lambda j, i: (i, j)),
        out_shape=jax.ShapeDtypeStruct((N_TOK, IN_COLS), F32),
        scratch_shapes=[pltpu.VMEM((PROJ_TN, D_MODEL), BF16)],
        compiler_params=_params("arbitrary", "arbitrary", vmem_limit=VMEM_LIMIT_BIG),
        name="in_proj",
    )(xb, w_in_t)


def _head_column(values, rows_per_head, n_rows):
    grp = lax.broadcasted_iota(jnp.int32, (n_rows, 1), 0) // rows_per_head
    col = jnp.full((n_rows, 1), values[-1], F32)
    for g in range(len(values) - 2, -1, -1):
        col = jnp.where(grp == g, values[g], col)
    return col


def _softmax_with_sink(s, sink_col):
    m = jnp.maximum(jnp.max(s, axis=-1, keepdims=True), sink_col)
    p = jnp.exp(s - m)
    denom = jnp.sum(p, axis=-1, keepdims=True) + jnp.exp(sink_col - m)
    return p, denom


def _attn_prompt_kernel(sink_ref, q_ref, kc_ref, kp_ref, vc_ref, vp_ref, o_ref):
    step = pl.program_id(1)
    n_rows = GQA * WINDOW
    qpos = lax.broadcasted_iota(jnp.int32, (n_rows, 2 * WINDOW), 0) % WINDOW
    kidx = lax.broadcasted_iota(jnp.int32, (n_rows, 2 * WINDOW), 1)
    dist_i = qpos + WINDOW - kidx
    in_window = (dist_i >= 0) & (dist_i <= WINDOW)
    dist = dist_i.astype(F32)
    for sub in range(ATTN_QB):
        rows = slice(sub * WINDOW, (sub + 1) * WINDOW)
        before = slice((sub - 1) * WINDOW, sub * WINDOW)
        valid = in_window if sub > 0 else in_window & ((kidx >= WINDOW) | (step > 0))
        for kvh in range(KV_HEADS):
            cols = slice(kvh * HEAD_DIM, (kvh + 1) * HEAD_DIM)
            k_before = kc_ref[before, cols] if sub > 0 else kp_ref[:, cols]
            v_before = vc_ref[before, cols] if sub > 0 else vp_ref[:, cols]
            kk = jnp.concatenate([k_before, kc_ref[rows, cols]], axis=0)
            vv = jnp.concatenate([v_before, vc_ref[rows, cols]], axis=0)
            heads = [kvh * GQA + g for g in range(GQA)]
            qs = jnp.concatenate([q_ref[rows, h * HEAD_DIM:(h + 1) * HEAD_DIM] for h in heads], axis=0)
            slope = _head_column([2.0 ** (-8.0 * (h + 1) / ATTN_HEADS) for h in heads], WINDOW, n_rows)
            sink = _head_column([sink_ref[h] for h in heads], WINDOW, n_rows)
            s = _mm_nt(qs, kk) * (HEAD_DIM ** -0.5) - slope * dist
            s = jnp.where(valid, s, NEG_INF)
            p, denom = _softmax_with_sink(s, sink)
            o = _mm(p, vv) / denom
            for g, h in enumerate(heads):
                o_ref[rows, h * HEAD_DIM:(h + 1) * HEAD_DIM] = o[g * WINDOW:(g + 1) * WINDOW].astype(BF16)


def _attn_prompt(proj, sinks):
    nb = SEQ // WINDOW
    ns = nb // ATTN_QB
    rows = ATTN_QB * WINDOW
    kcol, vcol = COL_K // KV_WIDTH, COL_V // KV_WIDTH

    def cur(c):
        return lambda b, i: (b * ns + i, c)

    def prev(c):
        return lambda b, i: (jnp.maximum(b * nb + i * ATTN_QB - 1, 0), c)

    return pl.pallas_call(
        _attn_prompt_kernel,
        grid=(BATCH, ns),
        in_specs=[pl.BlockSpec(memory_space=pltpu.SMEM),
                  pl.BlockSpec((rows, ATTN_WIDTH), cur(0)),
                  pl.BlockSpec((rows, KV_WIDTH), cur(kcol)),
                  pl.BlockSpec((WINDOW, KV_WIDTH), prev(kcol)),
                  pl.BlockSpec((rows, KV_WIDTH), cur(vcol)),
                  pl.BlockSpec((WINDOW, KV_WIDTH), prev(vcol))],
        out_specs=pl.BlockSpec((rows, ATTN_WIDTH), lambda b, i: (b * ns + i, 0)),
        out_shape=jax.ShapeDtypeStruct((N_PROMPT, ATTN_WIDTH), BF16),
        compiler_params=_params("arbitrary", "arbitrary"),
        name="attn_prompt",
    )(sinks, proj, proj, proj, proj, proj)


def _attn_sample_kernel(sink_ref, q_ref, kn_ref, vn_ref, kb_ref, vb_ref, o_ref, ko_ref, vo_ref):
    n_rows = GQA * DEC_SEQ
    tpos = lax.broadcasted_iota(jnp.int32, (n_rows, 2 * WINDOW), 0) % DEC_SEQ
    kidx = lax.broadcasted_iota(jnp.int32, (n_rows, 2 * WINDOW), 1)
    dist_i = jnp.where(kidx < WINDOW, tpos + WINDOW - kidx, tpos - (kidx - WINDOW))
    valid = (dist_i >= 0) & (dist_i <= WINDOW) & (kidx < WINDOW + DEC_SEQ)
    dist = dist_i.astype(F32)
    pad = jnp.zeros((WINDOW - DEC_SEQ, HEAD_DIM), F32)
    for bb in range(SAMPLE_BB):
        rows = slice(bb * DEC_SEQ, (bb + 1) * DEC_SEQ)
        for kvh in range(KV_HEADS):
            cols = slice(kvh * HEAD_DIM, (kvh + 1) * HEAD_DIM)
            kk = jnp.concatenate([kb_ref[bb, :, cols], kn_ref[rows, cols], pad], axis=0)
            vv = jnp.concatenate([vb_ref[bb, :, cols], vn_ref[rows, cols], pad], axis=0)
            heads = [kvh * GQA + g for g in range(GQA)]
            qs = jnp.concatenate([q_ref[rows, h * HEAD_DIM:(h + 1) * HEAD_DIM] for h in heads], axis=0)
            slope = _head_column([2.0 ** (-8.0 * (h + 1) / ATTN_HEADS) for h in heads], DEC_SEQ, n_rows)
            sink = _head_column([sink_ref[h] for h in heads], DEC_SEQ, n_rows)
            s = _mm_nt(qs, kk) * (HEAD_DIM ** -0.5) - slope * dist
            s = jnp.where(valid, s, NEG_INF)
            p, denom = _softmax_with_sink(s, sink)
            o = _mm(p, vv) / denom
            for g, h in enumerate(heads):
                o_ref[rows, h * HEAD_DIM:(h + 1) * HEAD_DIM] = o[g * DEC_SEQ:(g + 1) * DEC_SEQ]
        ko_ref[bb, 0:WINDOW - DEC_SEQ, :] = kb_ref[bb, DEC_SEQ:WINDOW, :]
        ko_ref[bb, WINDOW - DEC_SEQ:WINDOW, :] = kn_ref[rows, :]
        vo_ref[bb, 0:WINDOW - DEC_SEQ, :] = vb_ref[bb, DEC_SEQ:WINDOW, :]
        vo_ref[bb, WINDOW - DEC_SEQ:WINDOW, :] = vn_ref[rows, :]


def _attn_sample(proj, sinks, k_buf, v_buf, layer):
    rows = SAMPLE_BB * DEC_SEQ
    row0 = N_PROMPT // rows
    kcol, vcol = COL_K // KV_WIDTH, COL_V // KV_WIDTH
    cache_in = pl.BlockSpec((None, SAMPLE_BB, WINDOW, KV_WIDTH), lambda i: (layer, i, 0, 0))
    cache = pl.BlockSpec((SAMPLE_BB, WINDOW, KV_WIDTH), lambda i: (i, 0, 0))
    cache_shape = jax.ShapeDtypeStruct((DEC_BATCH, WINDOW, KV_WIDTH), F32)
    return pl.pallas_call(
        _attn_sample_kernel,
        grid=(DEC_BATCH // SAMPLE_BB,),
        in_specs=[pl.BlockSpec(memory_space=pltpu.SMEM),
                  pl.BlockSpec((rows, ATTN_WIDTH), lambda i: (row0 + i, 0)),
                  pl.BlockSpec((rows, KV_WIDTH), lambda i: (row0 + i, kcol)),
                  pl.BlockSpec((rows, KV_WIDTH), lambda i: (row0 + i, vcol)),
                  cache_in, cache_in],
        out_specs=[pl.BlockSpec((rows, ATTN_WIDTH), lambda i: (i, 0)), cache, cache],
        out_shape=[jax.ShapeDtypeStruct((N_SAMPLE, ATTN_WIDTH), F32), cache_shape, cache_shape],
        compiler_params=_params("arbitrary"),
        name="attn_sample",
    )(sinks, proj, proj, proj, k_buf, v_buf)


def _cumsum_rows(x):
    n = x.shape[0]
    row = lax.broadcasted_iota(jnp.int32, x.shape, 0)
    step = 1
    while step < n:
        x = x + jnp.where(row >= step, pltpu.roll(x, step, axis=0), 0.0)
        step *= 2
    return x


def _gdn_kernel(*refs, chunk, rows_in, has_state):
    if has_state:
        b1_ref, b2_ref, b3_ref, cw_ref, gp_ref, nw_ref, st_ref, s0_ref, o_ref, s_ref, ext_ref = refs
        first = True
    else:
        b1_ref, b2_ref, b3_ref, cw_ref, gp_ref, nw_ref, o_ref, s_ref, ext_ref = refs
        first = pl.program_id(1) == 0

    if has_state:
        ext_ref[0:8, :] = st_ref[...]
        s_ref[...] = s0_ref[...]
    else:
        @pl.when(first)
        def _():
            ext_ref[0:8, :] = jnp.zeros((8, CONV_CH), F32)
            s_ref[...] = jnp.zeros_like(s_ref)

        @pl.when(jnp.logical_not(first))
        def _():
            ext_ref[0:8, :] = ext_ref[chunk:chunk + 8, :]

    ext_ref[8:8 + rows_in, 0:GDN_COLS] = b1_ref[...]
    ext_ref[8:8 + rows_in, GDN_COLS:CONV_CH] = b2_ref[...]
    if rows_in < chunk:
        ext_ref[8 + rows_in:8 + chunk, :] = jnp.zeros((chunk - rows_in, CONV_CH), F32)

    lane = lax.broadcasted_iota(jnp.int32, (chunk, LANES), 1)
    row = lax.broadcasted_iota(jnp.int32, (chunk, LANES), 0)
    if rows_in < chunk:
        ab = jnp.concatenate([b3_ref[:, DN_VW:DN_VW + LANES],
                              jnp.zeros((chunk - rows_in, LANES), F32)], axis=0)
    else:
        ab = b3_ref[:, DN_VW:DN_VW + LANES]
    live = (lane < 2 * DN_HEADS) & (row < rows_in)
    ab = jnp.where(live, ab, 0.0)
    a_scale = gp_ref[0:1, :]
    dt_bias = gp_ref[1:2, :]
    g_all = jnp.where(live, a_scale * jax.nn.softplus(ab + dt_bias), 0.0)
    beta_all = jnp.where(live, jax.nn.sigmoid(ab), 0.0)
    gcum = _cumsum_rows(g_all)
    gcum_t = gcum.T
    g_last_all = gcum[chunk - 1:chunk, :]

    ri = lax.broadcasted_iota(jnp.int32, (chunk, chunk), 0)
    ci = lax.broadcasted_iota(jnp.int32, (chunk, chunk), 1)
    incl = ri >= ci
    strict = ri > ci

    def conv_strip(col0):
        cols = slice(col0, col0 + LANES)
        acc = ext_ref[8:8 + chunk, cols] * cw_ref[CONV_W - 1:CONV_W, cols]
        for tap in range(1, CONV_W):
            acc = acc + ext_ref[8 - tap:8 - tap + chunk, cols] * cw_ref[CONV_W - 1 - tap:CONV_W - tap, cols]
        return _silu(acc)

    def l2norm(x):
        return x * lax.rsqrt(jnp.sum(x * x, axis=-1, keepdims=True) + RMS_EPS)

    heads = range(DN_HEADS)
    q = [l2norm(conv_strip(h * DN_DK)) * (DN_DK ** -0.5) for h in heads]
    k = [l2norm(conv_strip(DN_KW + h * DN_DK)) for h in heads]
    v = [conv_strip(2 * DN_KW + h * DN_DV) for h in heads]
    gc_col = [gcum[:, h:h + 1] for h in heads]
    g_last = [g_last_all[:, h:h + 1] for h in heads]
    beta = [beta_all[:, DN_HEADS + h:DN_HEADS + h + 1] for h in heads]
    decay = [jnp.where(incl, jnp.exp(jnp.where(incl, gc_col[h] - gcum_t[h:h + 1, :], 0.0)), 0.0) for h in heads]
    eg = [jnp.exp(gc_col[h]) for h in heads]
    kb = [k[h] * beta[h] for h in heads]
    kk = [_mm_nt(jnp.concatenate([kb[h], q[h]], axis=0), k[h]) for h in heads]
    qk = [jnp.where(incl, kk[h][chunk:] * decay[h], 0.0) for h in heads]
    neg_a = [jnp.where(strict, -kk[h][:chunk] * decay[h], 0.0) for h in heads]
    base = min(GDN_INV_BASE, chunk)
    p = [jnp.where(ri // base == ci // base, neg_a[h], 0.0) for h in heads]
    eye = (ri == ci).astype(F32)
    inv = [eye + p[h] for h in heads]
    span = 2
    while span < base:
        p = [_mm(p[h], p[h]) for h in heads]
        inv = [inv[h] + _mm(inv[h], p[h]) for h in heads]
        span *= 2
    size = base
    while size < chunk:
        lower_left = (ri // (2 * size) == ci // (2 * size)) & (ri // size != ci // size)
        off = [jnp.where(lower_left, neg_a[h], 0.0) for h in heads]
        inv = [inv[h] + _mm(_mm(inv[h], off[h]), inv[h]) for h in heads]
        size *= 2
    sol = [_mm(inv[h], jnp.concatenate([v[h] * beta[h], kb[h] * eg[h]], axis=1)) for h in heads]
    s_prev = [s_ref[h] for h in heads]
    ws = [_mm(jnp.concatenate([sol[h][:, DN_DV:], q[h] * eg[h]], axis=0), s_prev[h]) for h in heads]
    v_new = [sol[h][:, :DN_DV] - ws[h][:chunk] for h in heads]
    o = [ws[h][chunk:] + _mm(qk[h], v_new[h]) for h in heads]
    k_dec = [k[h] * jnp.exp(g_last[h] - gc_col[h]) for h in heads]
    for h in heads:
        s_ref[h] = s_prev[h] * jnp.exp(g_last[h]) + _mm_tn(k_dec[h], v_new[h])
    for h in heads:
        on = o[h] * lax.rsqrt(jnp.mean(o[h] * o[h], axis=-1, keepdims=True) + RMS_EPS)
        z = b3_ref[:, h * DN_DV:(h + 1) * DN_DV]
        out = on[0:rows_in] * nw_ref[...] * _silu(z)
        o_ref[:, h * DN_DV:(h + 1) * DN_DV] = out.astype(o_ref.dtype)


def _gdn_gate_params(a_log, dt_bias):
    rows = jnp.stack([-jnp.exp(a_log.astype(F32)), dt_bias.astype(F32)], axis=1)
    return jnp.pad(rows, ((0, 0), (0, 0), (0, LANES - DN_HEADS)))


def _gdn_prompt(proj, conv_w, gp, norm_w, layer):
    nc = SEQ // GDN_CHUNK
    c0 = COL_QKV // GDN_COLS

    def win(c):
        return pl.BlockSpec((GDN_CHUNK, GDN_COLS), lambda b, i: (b * nc + i, c))

    return pl.pallas_call(
        functools.partial(_gdn_kernel, chunk=GDN_CHUNK, rows_in=GDN_CHUNK, has_state=False),
        grid=(BATCH, nc),
        in_specs=[win(c0), win(c0 + 1), win(c0 + 2), _layer_slab((CONV_W, CONV_CH), layer),
                  _layer_slab((2, LANES), layer), _layer_slab((1, DN_DV), layer)],
        out_specs=[pl.BlockSpec((GDN_CHUNK, DN_VW), lambda b, i: (b * nc + i, 0)),
                   pl.BlockSpec((None, DN_HEADS, DN_DK, DN_DV), lambda b, i: (b, 0, 0, 0))],
        out_shape=[jax.ShapeDtypeStruct((N_PROMPT, DN_VW), BF16),
                   jax.ShapeDtypeStruct((BATCH, DN_HEADS, DN_DK, DN_DV), F32)],
        scratch_shapes=[pltpu.VMEM((8 + GDN_CHUNK, CONV_CH), F32)],
        compiler_params=_params("arbitrary", "arbitrary"),
        name="gdn_prompt",
    )(proj, proj, proj, conv_w, gp, norm_w)


def _gdn_sample(proj, conv_w, gp, norm_w, conv_state8, s0, chunk, layer):
    row0 = N_PROMPT // DEC_SEQ
    c0 = COL_QKV // GDN_COLS

    def win(c):
        return pl.BlockSpec((DEC_SEQ, GDN_COLS), lambda b: (row0 + b, c))

    state = pl.BlockSpec((None, DN_HEADS, DN_DK, DN_DV), lambda b: (b, 0, 0, 0))
    return pl.pallas_call(
        functools.partial(_gdn_kernel, chunk=chunk, rows_in=DEC_SEQ, has_state=True),
        grid=(DEC_BATCH,),
        in_specs=[win(c0), win(c0 + 1), win(c0 + 2), _layer_slab((CONV_W, CONV_CH), layer),
                  _layer_slab((2, LANES), layer), _layer_slab((1, DN_DV), layer),
                  pl.BlockSpec((None, None, 8, CONV_CH), lambda b: (layer, b, 0, 0)),
                  pl.BlockSpec((None, None, DN_HEADS, DN_DK, DN_DV), lambda b: (layer, b, 0, 0, 0))],
        out_specs=[pl.BlockSpec((DEC_SEQ, DN_VW), lambda b: (b, 0)), state],
        out_shape=[jax.ShapeDtypeStruct((N_SAMPLE, DN_VW), F32),
                   jax.ShapeDtypeStruct((DEC_BATCH, DN_HEADS, DN_DK, DN_DV), F32)],
        scratch_shapes=[pltpu.VMEM((8 + chunk, CONV_CH), F32)],
        compiler_params=_params("arbitrary"),
        name="gdn_sample",
    )(proj, proj, proj, conv_w, gp, norm_w, conv_state8, s0)


def _out_router_kernel(map_ref, mas_ref, mdp_ref, mds_ref, x_ref, wo_ref, g_ref, b_ref, wr_ref, br_ref,
                       x1_ref, eid_ref, gate_ref):
    def emit(ma_ref, md_ref):
        mix = (jnp.dot(ma_ref[...].astype(BF16), wo_ref[0:ATTN_WIDTH, :], preferred_element_type=F32)
               + jnp.dot(md_ref[...].astype(BF16), wo_ref[ATTN_WIDTH:MIX_WIDTH, :], preferred_element_type=F32))
        x1 = _layer_norm(DEEPNORM_ALPHA * x_ref[...] + mix, g_ref[...], b_ref[...])
        x1_ref[...] = x1
        logits = jnp.dot(x1.astype(BF16), wr_ref[...], preferred_element_type=F32) + br_ref[...]
        eid, gate = _route(logits)
        eid_ref[...] = eid
        gate_ref[...] = gate

    is_prompt = pl.program_id(0) < N_PROMPT // ROW_TILE
    pl.when(is_prompt)(lambda: emit(map_ref, mdp_ref))
    pl.when(jnp.logical_not(is_prompt))(lambda: emit(mas_ref, mds_ref))


def _route(logits):
    lane = lax.broadcasted_iota(jnp.int32, logits.shape, 1)
    lane_f = lane.astype(F32)
    far = float(LANES)
    is_grp = lane < N_GROUPS
    grp_max = jnp.max(jnp.where(is_grp, logits, NEG_INF), axis=-1, keepdims=True)
    grp = jnp.min(jnp.where(is_grp & (logits == grp_max), lane_f, far), axis=-1, keepdims=True)
    grp_gate = 1.0 / jnp.sum(jnp.where(is_grp, jnp.exp(logits - grp_max), 0.0), axis=-1, keepdims=True)
    lo = N_GROUPS + EXPERTS_PER_GROUP * grp
    in_grp = (lane_f >= lo) & (lane_f < lo + EXPERTS_PER_GROUP)
    v1 = jnp.max(jnp.where(in_grp, logits, NEG_INF), axis=-1, keepdims=True)
    i1 = jnp.min(jnp.where(in_grp & (logits == v1), lane_f, far), axis=-1, keepdims=True)
    rest = in_grp & (lane_f != i1)
    v2 = jnp.max(jnp.where(rest, logits, NEG_INF), axis=-1, keepdims=True)
    i2 = jnp.min(jnp.where(rest & (logits == v2), lane_f, far), axis=-1, keepdims=True)
    t = jnp.exp(v2 - v1)
    g1 = grp_gate / (1.0 + t)
    g2 = g1 * t
    e1 = (i1 - N_GROUPS).astype(jnp.int32)
    e2 = (i2 - N_GROUPS).astype(jnp.int32)
    return (jnp.where(lane == 0, e1, jnp.where(lane == 1, e2, 0)),
            jnp.where(lane == 0, g1, jnp.where(lane == 1, g2, 0.0)))


def _out_router(mix_a_p, mix_a_s, mix_d_p, mix_d_s, x, wo_b, g, b, wr_b, br, layer):
    assert N_SAMPLE == ROW_TILE

    def row(w):
        return pl.BlockSpec((ROW_TILE, w), lambda i: (i, 0))

    last_prompt = N_PROMPT // ROW_TILE - 1
    prompt = pl.BlockSpec((ROW_TILE, ATTN_WIDTH), lambda i: (jnp.minimum(i, last_prompt), 0))
    sample = pl.BlockSpec((ROW_TILE, ATTN_WIDTH), lambda i: (0, 0))
    return pl.pallas_call(
        _out_router_kernel,
        grid=(N_TOK // ROW_TILE,),
        in_specs=[prompt, sample, prompt, sample, row(D_MODEL), _layer_slab((MIX_WIDTH, D_MODEL), layer),
                  _layer_slab((1, D_MODEL), layer), _layer_slab((1, D_MODEL), layer),
                  _layer_slab((D_MODEL, LANES), layer), _layer_slab((1, LANES), layer)],
        out_specs=[row(D_MODEL), row(LANES), row(LANES)],
        out_shape=[jax.ShapeDtypeStruct((N_TOK, D_MODEL), F32),
                   jax.ShapeDtypeStruct((N_TOK, LANES), jnp.int32),
                   jax.ShapeDtypeStruct((N_TOK, LANES), F32)],
        compiler_params=_params("arbitrary"),
        name="out_router",
    )(mix_a_p, mix_a_s, mix_d_p, mix_d_s, x, wo_b, g, b, wr_b, br)


def _plan_kernel(eid_ref, pos_ref, cnt_ref, run_ref, off_ref):
    phase = pl.program_id(0)
    i = pl.program_id(1)
    lane = lax.broadcasted_iota(jnp.int32, (ROW_TILE, LANES), 1)
    eid = eid_ref[...]
    hot0 = (lane == eid[:, 0:1]).astype(F32)
    hot1 = (lane == eid[:, 1:2]).astype(F32)
    hot = hot0 + hot1
    tile_cnt = jnp.sum(hot, axis=0, keepdims=True)

    @pl.when((phase == 0) & (i == 0))
    def _():
        cnt_ref[...] = jnp.zeros_like(cnt_ref)

    @pl.when(phase == 0)
    def _():
        cnt_ref[...] += tile_cnt

    @pl.when((phase == 1) & (i == 0))
    def _():
        cnt = cnt_ref[...]
        hi = jnp.floor(cnt * (1.0 / 256.0))
        lo = cnt - 256.0 * hi
        r = lax.broadcasted_iota(jnp.int32, (LANES, LANES), 0)
        c = lax.broadcasted_iota(jnp.int32, (LANES, LANES), 1)
        upper = (r < c).astype(F32)
        hi8 = jnp.broadcast_to(hi, (8, LANES))
        lo8 = jnp.broadcast_to(lo, (8, LANES))
        off = 256.0 * _mm(hi8, upper) + _mm(lo8, upper)
        off_ref[...] = off[0:1]
        run_ref[...] = jnp.zeros_like(run_ref)

    @pl.when(phase == 1)
    def _():
        r = lax.broadcasted_iota(jnp.int32, (ROW_TILE, ROW_TILE), 0)
        c = lax.broadcasted_iota(jnp.int32, (ROW_TILE, ROW_TILE), 1)
        before = (r > c).astype(F32)
        slot = _mm(before, hot) + run_ref[...] + off_ref[...]
        p0 = jnp.sum(hot0 * slot, axis=-1, keepdims=True).astype(jnp.int32)
        p1 = jnp.sum(hot1 * slot, axis=-1, keepdims=True).astype(jnp.int32)
        pos_ref[...] = jnp.where(lane == 0, p0, jnp.where(lane == 1, p1, 0))
        run_ref[...] += tile_cnt


def _plan(eid):
    return pl.pallas_call(
        _plan_kernel,
        grid=(2, N_TOK // ROW_TILE),
        in_specs=[pl.BlockSpec((ROW_TILE, LANES), lambda p, i: (i, 0))],
        out_specs=[pl.BlockSpec((ROW_TILE, LANES), lambda p, i: (i * p, 0)),
                   pl.BlockSpec((1, LANES), lambda p, i: (0, 0))],
        out_shape=[jax.ShapeDtypeStruct((N_TOK, LANES), jnp.int32),
                   jax.ShapeDtypeStruct((1, LANES), F32)],
        scratch_shapes=[pltpu.VMEM((1, LANES), F32), pltpu.VMEM((1, LANES), F32)],
        compiler_params=_params("arbitrary", "arbitrary"),
        name="moe_plan",
    )(eid)


def _work_items(counts):
    counts = counts.astype(jnp.int32)
    ends = jnp.cumsum(counts)
    starts = ends - counts
    first_tile = starts // MOE_TM
    n_tiles = jnp.where(counts > 0, (ends - 1) // MOE_TM - first_tile + 1, 0)
    item_end = jnp.cumsum(n_tiles)
    item_start = item_end - n_tiles
    n_items = item_end[-1]
    w = jnp.minimum(jnp.arange(N_MOE_ITEMS, dtype=jnp.int32), n_items - 1)
    expert = jnp.sum((item_end[None, :] <= w[:, None]).astype(jnp.int32), axis=1)
    tile = first_tile[expert] + (w - item_start[expert])
    prev_tile = jnp.concatenate([jnp.full((1,), -1, jnp.int32), tile[:-1]])
    prev_expert = jnp.concatenate([jnp.full((1,), -1, jnp.int32), expert[:-1]])
    valid = (jnp.arange(N_MOE_ITEMS, dtype=jnp.int32) < n_items).astype(jnp.int32)
    return (tile, expert, (tile != prev_tile).astype(jnp.int32), (expert != prev_expert).astype(jnp.int32),
            valid, starts[expert], ends[expert])


def _slot_tokens_kernel(pos_ref, tok_ref):
    group = 16

    def place(g, carry):
        a0 = g * group
        slots = [pos_ref[a0 + j] for j in range(group)]
        for j in range(group):
            tok_ref[slots[j]] = g * (group // 2) + j // 2
        return carry

    lax.fori_loop(0, N_ASSIGN // group, place, 0)


def _slot_tokens(pos_flat):
    return pl.pallas_call(
        _slot_tokens_kernel,
        grid_spec=pltpu.PrefetchScalarGridSpec(
            num_scalar_prefetch=1,
            grid=(1,),
            in_specs=[],
            out_specs=pl.BlockSpec(memory_space=pltpu.SMEM)),
        out_shape=jax.ShapeDtypeStruct((N_ASSIGN,), jnp.int32),
        compiler_params=_params("arbitrary"),
        name="moe_slot_tokens",
    )(pos_flat)


def _experts_kernel(tile_ref, exp_ref, first_ref, newexp_ref, valid_ref, gstart_ref, gend_ref, tok_ref,
                    x1_ref, wgu_ref, wdn_ref, o_ref, wgu_b, wdn_b, xs_ref, sem):
    w = pl.program_id(0)

    def start_gather(tile, into):
        base = tile * MOE_TM

        def issue(r, carry):
            pltpu.make_async_copy(x1_ref.at[pl.ds(tok_ref[base + r], 1), :],
                                  xs_ref.at[into, pl.ds(r, 1), :], sem.at[into]).start()
            return carry

        lax.fori_loop(0, MOE_TM, issue, 0, unroll=8)

    @pl.when(w == 0)
    def _():
        start_gather(0, 0)

    @pl.when(valid_ref[w] == 1)
    def _():
        tile = tile_ref[w]
        buf = tile % 2

        @pl.when(first_ref[w] == 1)
        def _():
            pltpu.make_async_copy(x1_ref.at[pl.ds(0, MOE_TM), :], xs_ref.at[buf], sem.at[buf]).wait()

            @pl.when(tile + 1 < N_MOE_TILES)
            def _():
                start_gather(tile + 1, 1 - buf)

        @pl.when(newexp_ref[w] == 1)
        def _():
            wgu_b[...] = wgu_ref[...].astype(BF16)
            wdn_b[...] = wdn_ref[...].astype(BF16)

        h = jnp.dot(xs_ref[buf].astype(BF16), wgu_b[...], preferred_element_type=F32)
        act = _silu(h[:, :D_FF_EXPERT]) * h[:, D_FF_EXPERT:]
        y = jnp.dot(act.astype(BF16), wdn_b[...], preferred_element_type=F32)
        rows = tile_ref[w] * MOE_TM + lax.broadcasted_iota(jnp.int32, (MOE_TM, 1), 0)
        mine = (rows >= gstart_ref[w]) & (rows < gend_ref[w])

        @pl.when(first_ref[w] == 1)
        def _():
            o_ref[...] = jnp.where(mine, y, 0.0)

        @pl.when(first_ref[w] == 0)
        def _():
            o_ref[...] = jnp.where(mine, y, o_ref[...])


def _experts(items, slot_tokens, x1, w_gu, w_dn, layer):
    return pl.pallas_call(
        _experts_kernel,
        grid_spec=pltpu.PrefetchScalarGridSpec(
            num_scalar_prefetch=8,
            grid=(N_MOE_ITEMS,),
            in_specs=[pl.BlockSpec(memory_space=pl.ANY),
                      pl.BlockSpec((None, None, D_MODEL, 2 * D_FF_EXPERT),
                                   lambda w, t, e, *_: (layer, e[w], 0, 0)),
                      pl.BlockSpec((None, None, D_FF_EXPERT, D_MODEL),
                                   lambda w, t, e, *_: (layer, e[w], 0, 0))],
            out_specs=pl.BlockSpec((MOE_TM, D_MODEL), lambda w, t, e, *_: (t[w], 0)),
            scratch_shapes=[pltpu.VMEM((D_MODEL, 2 * D_FF_EXPERT), BF16),
                            pltpu.VMEM((D_FF_EXPERT, D_MODEL), BF16),
                            pltpu.VMEM((2, MOE_TM, D_MODEL), F32),
                            pltpu.SemaphoreType.DMA((2,))]),
        out_shape=jax.ShapeDtypeStruct((N_ASSIGN, D_MODEL), F32),
        compiler_params=_params("arbitrary"),
        name="moe_experts",
    )(*items, slot_tokens, x1, w_gu, w_dn)


def _combine_kernel(pos_ref, ys_ref, x1_ref, gate_ref, g_ref, b_ref, out_a_ref, out_b_ref, rows_ref, sem, *,
                    last_layer):
    i = pl.program_id(0)
    buf = i % 2

    def start_gather(tile, into):
        base = tile * (2 * ROW_TILE)

        def issue(r, carry):
            for pick in range(2):
                pltpu.make_async_copy(ys_ref.at[pl.ds(pos_ref[base + 2 * r + pick], 1), :],
                                      rows_ref.at[into, pick, pl.ds(r, 1), :], sem.at[into]).start()
            return carry

        lax.fori_loop(0, ROW_TILE, issue, 0, unroll=8)

    @pl.when(i == 0)
    def _():
        start_gather(0, 0)

    @pl.when(i + 1 < pl.num_programs(0))
    def _():
        start_gather(i + 1, 1 - buf)

    for pick in range(2):
        pltpu.make_async_copy(ys_ref.at[pl.ds(0, ROW_TILE), :], rows_ref.at[buf, pick], sem.at[buf]).wait()
    gate = gate_ref[...]
    y = gate[:, 0:1] * rows_ref[buf, 0] + gate[:, 1:2] * rows_ref[buf, 1]
    x2 = _layer_norm(DEEPNORM_ALPHA * x1_ref[...] + y, g_ref[...], b_ref[...])
    if last_layer:
        is_prompt = i < N_PROMPT // ROW_TILE

        @pl.when(is_prompt)
        def _():
            out_a_ref[...] = x2

        @pl.when(jnp.logical_not(is_prompt))
        def _():
            out_b_ref[...] = x2
    else:
        out_a_ref[...] = x2
        out_b_ref[...] = x2.astype(BF16)


def _combine(pos_flat, ys, x1, gates, g, b, layer):
    row = lambda w: pl.BlockSpec((ROW_TILE, w), lambda i, pos: (i, 0))
    vec = _layer_slab((1, D_MODEL), layer)
    last_layer = layer == DEPTH - 1
    if last_layer:
        assert N_SAMPLE == ROW_TILE
        last_prompt = N_PROMPT // ROW_TILE - 1
        out_specs = [pl.BlockSpec((ROW_TILE, D_MODEL), lambda i, pos: (jnp.minimum(i, last_prompt), 0)),
                     pl.BlockSpec((ROW_TILE, D_MODEL), lambda i, pos: (0, 0))]
        out_shape = [jax.ShapeDtypeStruct((N_PROMPT, D_MODEL), F32), jax.ShapeDtypeStruct((N_SAMPLE, D_MODEL), F32)]
    else:
        out_specs = [row(D_MODEL), row(D_MODEL)]
        out_shape = [jax.ShapeDtypeStruct((N_TOK, D_MODEL), F32), jax.ShapeDtypeStruct((N_TOK, D_MODEL), BF16)]
    return pl.pallas_call(
        functools.partial(_combine_kernel, last_layer=last_layer),
        grid_spec=pltpu.PrefetchScalarGridSpec(
            num_scalar_prefetch=1,
            grid=(N_TOK // ROW_TILE,),
            in_specs=[pl.BlockSpec(memory_space=pl.ANY), row(D_MODEL), row(LANES), vec, vec],
            out_specs=out_specs,
            scratch_shapes=[pltpu.VMEM((2, 2, ROW_TILE, D_MODEL), F32),
                            pltpu.SemaphoreType.DMA((2,))]),
        out_shape=out_shape,
        compiler_params=_params("arbitrary"),
        name="moe_combine",
    )(pos_flat, ys, x1, gates, g, b)


def _router_weights(wg, bg, we, be):
    pad = LANES - N_GROUPS - N_EXPERTS
    wr = jnp.concatenate([wg, we, jnp.zeros((DEPTH, D_MODEL, pad), wg.dtype)], axis=-1)
    br = jnp.concatenate([bg, be, jnp.zeros((DEPTH, pad), bg.dtype)], axis=-1)
    return wr.astype(BF16), br.astype(F32).reshape(DEPTH, 1, LANES)


def kernel(x_prompt, x_sample, cache_swa_k, cache_swa_v, state_conv, state_delta, ln_in_g, ln_in_b, w_in, conv_w,
           a_log, dt_bias, dn_norm_w, attn_sinks, w_out, ln1_g, ln1_b, router_group_w, router_group_b,
           router_expert_w, router_expert_b, w_gate_up, w_down, ln2_g, ln2_b):
    x, xb = _ln_in(x_prompt.reshape(N_PROMPT, D_MODEL), x_sample.reshape(N_SAMPLE, D_MODEL), ln_in_g, ln_in_b)
    k_buf = cache_swa_k.reshape(DEPTH, DEC_BATCH, WINDOW, KV_WIDTH)
    v_buf = cache_swa_v.reshape(DEPTH, DEC_BATCH, WINDOW, KV_WIDTH)
    conv_state8 = jnp.pad(state_conv, ((0, 0), (0, 0), (8 - (CONV_W - 1), 0), (0, 0)))
    gp = _gdn_gate_params(a_log, dt_bias)
    norm_w = dn_norm_w.reshape(DEPTH, 1, DN_DV).astype(F32)
    wo_b = w_out.astype(BF16)
    w_in_t = jnp.swapaxes(w_in, 1, 2)
    wr_b, br = _router_weights(router_group_w, router_group_b, router_expert_w, router_expert_b)
    ln1_g, ln1_b, ln2_g, ln2_b = (a.reshape(DEPTH, 1, D_MODEL) for a in (ln1_g, ln1_b, ln2_g, ln2_b))
    kp, vp, cp, sp = [], [], [], []
    kq, vq, cq, sq = [], [], [], []
    for l in range(DEPTH):
        proj = _proj(xb, w_in_t, l)
        sinks = attn_sinks[l].astype(F32)
        o_attn_p = _attn_prompt(proj, sinks)
        o_attn_s, k_s, v_s = _attn_sample(proj, sinks, k_buf, v_buf, l)
        o_dn_p, s_p = _gdn_prompt(proj, conv_w, gp, norm_w, l)
        o_dn_s, s_s = _gdn_sample(proj, conv_w, gp, norm_w, conv_state8, state_delta, GDN_CHUNK_SAMPLE, l)
        x1, eid, gates = _out_router(o_attn_p, o_attn_s, o_dn_p, o_dn_s, x, wo_b, ln1_g, ln1_b, wr_b, br, l)
        pos, counts = _plan(eid)
        pos_flat = pos[:, :2].reshape(N_ASSIGN)
        items = _work_items(counts[0, :N_EXPERTS])
        ys = _experts(items, _slot_tokens(pos_flat), x1, w_gate_up, w_down, l)
        x, xb = _combine(pos_flat, ys, x1, gates, ln2_g, ln2_b, l)

        def prompt_tail(rows, c0, c1):
            return jnp.stack([proj[(b + 1) * SEQ - rows:(b + 1) * SEQ, c0:c1] for b in range(BATCH)])

        kp.append(prompt_tail(WINDOW, COL_K, COL_V).reshape(BATCH, WINDOW, KV_HEADS, HEAD_DIM))
        vp.append(prompt_tail(WINDOW, COL_V, COL_QKV).reshape(BATCH, WINDOW, KV_HEADS, HEAD_DIM))
        cp.append(prompt_tail(CONV_W - 1, COL_QKV, COL_Z))
        sp.append(s_p)
        kq.append(k_s.reshape(DEC_BATCH, WINDOW, KV_HEADS, HEAD_DIM))
        vq.append(v_s.reshape(DEC_BATCH, WINDOW, KV_HEADS, HEAD_DIM))
        cq.append(proj[N_PROMPT:, COL_QKV:COL_Z].reshape(DEC_BATCH, DEC_SEQ, CONV_CH)[:, DEC_SEQ - (CONV_W - 1):])
        sq.append(s_s)
    return (x.reshape(BATCH, SEQ, D_MODEL), xb.reshape(DEC_BATCH, DEC_SEQ, D_MODEL),
            jnp.stack(kp), jnp.stack(vp), jnp.stack(cp), jnp.stack(sp),
            jnp.stack(kq), jnp.stack(vq), jnp.stack(cq), jnp.stack(sq))
```

```python
import functools

import jax
import jax.numpy as jnp
from jax import lax
from jax.experimental import pallas as pl
from jax.experimental.pallas import tpu as pltpu

D_MODEL = 2048
BATCH = 4
SEQ = 2048
DEPTH = 4
DEC_BATCH = 32
DEC_SEQ = 8
HEAD_DIM = 128
ATTN_HEADS = 8
KV_HEADS = 2
GQA = ATTN_HEADS // KV_HEADS
ATTN_WIDTH = ATTN_HEADS * HEAD_DIM
KV_WIDTH = KV_HEADS * HEAD_DIM
WINDOW = 128
DN_HEADS = 8
DN_DK = 128
DN_DV = 128
DN_KW = DN_HEADS * DN_DK
DN_VW = DN_HEADS * DN_DV
CONV_W = 4
CONV_CH = 2 * DN_KW + DN_VW
MIX_WIDTH = ATTN_WIDTH + DN_VW
IN_COLS = ATTN_WIDTH + 2 * KV_WIDTH + CONV_CH + DN_VW + 2 * DN_HEADS
N_GROUPS = 4
EXPERTS_PER_GROUP = 8
N_EXPERTS = N_GROUPS * EXPERTS_PER_GROUP
D_FF_EXPERT = 256
DEEPNORM_ALPHA = (2 * DEPTH) ** 0.25
LN_EPS = 1e-5
RMS_EPS = 1e-6
NEG_INF = -1e30

N_PROMPT = BATCH * SEQ
N_SAMPLE = DEC_BATCH * DEC_SEQ
N_TOK = N_PROMPT + N_SAMPLE
N_ASSIGN = 2 * N_TOK

COL_K = ATTN_WIDTH
COL_V = COL_K + KV_WIDTH
COL_QKV = COL_V + KV_WIDTH
COL_Z = COL_QKV + CONV_CH
COL_AB = COL_Z + DN_VW

LANES = 128
ROW_TILE = 256
PROJ_TM = 1408
PROJ_TN = 1024
OUT_TM = 768
OUT_SUB = 256
GDN_CHUNK = 128
GDN_CHUNK_SAMPLE = 16
GDN_HEAD_GROUP = 8
GDN_INV_BASE = 16
GDN_COLS = 1536
SAMPLE_BB = 4
ATTN_QB = 4
MOE_TM = 256
N_MOE_TILES = N_ASSIGN // MOE_TM
N_MOE_ITEMS = N_MOE_TILES + N_EXPERTS - 1
VMEM_LIMIT = 48 * 1024 * 1024
VMEM_LIMIT_BIG = 56 * 1024 * 1024

F32 = jnp.float32
BF16 = jnp.bfloat16


def _params(*sem, vmem_limit=VMEM_LIMIT):
    return pltpu.CompilerParams(dimension_semantics=sem, vmem_limit_bytes=vmem_limit)


def _mm(a, b):
    return jnp.dot(a.astype(BF16), b.astype(BF16), preferred_element_type=F32)


def _mm_nt(a, b):
    return lax.dot_general(a.astype(BF16), b.astype(BF16), (((1,), (1,)), ((), ())),
                           preferred_element_type=F32)


def _mm_tn(a, b):
    return lax.dot_general(a.astype(BF16), b.astype(BF16), (((0,), (0,)), ((), ())),
                           preferred_element_type=F32)


def _layer_slab(shape, layer):
    return pl.BlockSpec((None,) + tuple(shape), lambda *_: (layer,) + (0,) * len(shape),
                        pipeline_mode=pl.Buffered(1))


def _layer_norm(x, g, b):
    mu = jnp.mean(x, axis=-1, keepdims=True)
    xc = x - mu
    var = jnp.mean(xc * xc, axis=-1, keepdims=True)
    return xc * lax.rsqrt(var + LN_EPS) * g + b


def _silu(x):
    return x / (1.0 + jnp.exp(-x))


def _ln_in_kernel(xp_ref, xs_ref, g_ref, b_ref, o_ref, ob_ref):
    def emit(x_ref):
        y = _layer_norm(x_ref[...], g_ref[...], b_ref[...])
        o_ref[...] = y
        ob_ref[...] = y.astype(BF16)

    is_prompt = pl.program_id(0) < N_PROMPT // ROW_TILE
    pl.when(is_prompt)(lambda: emit(xp_ref))
    pl.when(jnp.logical_not(is_prompt))(lambda: emit(xs_ref))


def _ln_in(xp, xs, g, b):
    assert N_SAMPLE == ROW_TILE
    row = pl.BlockSpec((ROW_TILE, D_MODEL), lambda i: (i, 0))
    vec = pl.BlockSpec((1, D_MODEL), lambda i: (0, 0))
    last_prompt = N_PROMPT // ROW_TILE - 1
    return pl.pallas_call(
        _ln_in_kernel,
        grid=(N_TOK // ROW_TILE,),
        in_specs=[pl.BlockSpec((ROW_TILE, D_MODEL), lambda i: (jnp.minimum(i, last_prompt), 0)),
                  pl.BlockSpec((ROW_TILE, D_MODEL), lambda i: (0, 0)), vec, vec],
        out_specs=[row, row],
        out_shape=[jax.ShapeDtypeStruct((N_TOK, D_MODEL), F32),
                   jax.ShapeDtypeStruct((N_TOK, D_MODEL), BF16)],
        compiler_params=_params("arbitrary"),
        name="ln_in",
    )(xp, xs, g.reshape(1, D_MODEL), b.reshape(1, D_MODEL))


def _proj_kernel(x_ref, wt_ref, o_ref, wb_ref):
    @pl.when(pl.program_id(1) == 0)
    def _():
        wb_ref[...] = wt_ref[...].astype(BF16)

    o_ref[...] = lax.dot_general(x_ref[...], wb_ref[...], (((1,), (1,)), ((), ())), preferred_element_type=F32)


def _proj(xb, w_in_t, layer):
    return pl.pallas_call(
        _proj_kernel,
        grid=(pl.cdiv(IN_COLS, PROJ_TN), N_TOK // PROJ_TM),
        in_specs=[pl.BlockSpec((PROJ_TM, D_MODEL), lambda j, i: (i, 0)),
                  pl.BlockSpec((None, PROJ_TN, D_MODEL), lambda j, i: (layer, j, 0))],
        out_specs=pl.BlockSpec((PROJ_TM, PROJ_TN), lambda j, i: (i, j)),
        out_shape=jax.ShapeDtypeStruct((N_TOK, IN_COLS), F32),
        scratch_shapes=[pltpu.VMEM((PROJ_TN, D_MODEL), BF16)],
        compiler_params=_params("arbitrary", "arbitrary", vmem_limit=VMEM_LIMIT_BIG),
        name="in_proj",
    )(xb, w_in_t)


def _head_column(values, rows_per_head, n_rows):
    grp = lax.broadcasted_iota(jnp.int32, (n_rows, 1), 0) // rows_per_head
    col = jnp.full((n_rows, 1), values[-1], F32)
    for g in range(len(values) - 2, -1, -1):
        col = jnp.where(grp == g, values[g], col)
    return col


def _softmax_with_sink(s, sink_col):
    m = jnp.maximum(jnp.max(s, axis=-1, keepdims=True), sink_col)
    p = jnp.exp(s - m)
    denom = jnp.sum(p, axis=-1, keepdims=True) + jnp.exp(sink_col - m)
    return p, denom


def _attn_prompt_kernel(sink_ref, q_ref, kc_ref, kp_ref, vc_ref, vp_ref, o_ref):
    step = pl.program_id(1)
    n_rows = GQA * WINDOW
    qpos = lax.broadcasted_iota(jnp.int32, (n_rows, 2 * WINDOW), 0) % WINDOW
    kidx = lax.broadcasted_iota(jnp.int32, (n_rows, 2 * WINDOW), 1)
    dist_i = qpos + WINDOW - kidx
    in_window = (dist_i >= 0) & (dist_i <= WINDOW)
    dist = dist_i.astype(F32)
    for sub in range(ATTN_QB):
        rows = slice(sub * WINDOW, (sub + 1) * WINDOW)
        before = slice((sub - 1) * WINDOW, sub * WINDOW)
        valid = in_window if sub > 0 else in_window & ((kidx >= WINDOW) | (step > 0))
        for kvh in range(KV_HEADS):
            cols = slice(kvh * HEAD_DIM, (kvh + 1) * HEAD_DIM)
            k_before = kc_ref[before, cols] if sub > 0 else kp_ref[:, cols]
            v_before = vc_ref[before, cols] if sub > 0 else vp_ref[:, cols]
            kk = jnp.concatenate([k_before, kc_ref[rows, cols]], axis=0)
            vv = jnp.concatenate([v_before, vc_ref[rows, cols]], axis=0)
            heads = [kvh * GQA + g for g in range(GQA)]
            qs = jnp.concatenate([q_ref[rows, h * HEAD_DIM:(h + 1) * HEAD_DIM] for h in heads], axis=0)
            slope = _head_column([2.0 ** (-8.0 * (h + 1) / ATTN_HEADS) for h in heads], WINDOW, n_rows)
            sink = _head_column([sink_ref[h] for h in heads], WINDOW, n_rows)
            s = _mm_nt(qs, kk) * (HEAD_DIM ** -0.5) - slope * dist
            s = jnp.where(valid, s, NEG_INF)
            p, denom = _softmax_with_sink(s, sink)
            o = _mm(p, vv) / denom
            for g, h in enumerate(heads):
                o_ref[rows, h * HEAD_DIM:(h + 1) * HEAD_DIM] = o[g * WINDOW:(g + 1) * WINDOW].astype(BF16)


def _attn_prompt(proj, sinks):
    nb = SEQ // WINDOW
    ns = nb // ATTN_QB
    rows = ATTN_QB * WINDOW
    kcol, vcol = COL_K // KV_WIDTH, COL_V // KV_WIDTH

    def cur(c):
        return lambda b, i: (b * ns + i, c)

    def prev(c):
        return lambda b, i: (jnp.maximum(b * nb + i * ATTN_QB - 1, 0), c)

    return pl.pallas_call(
        _attn_prompt_kernel,
        grid=(BATCH, ns),
        in_specs=[pl.BlockSpec(memory_space=pltpu.SMEM),
                  pl.BlockSpec((rows, ATTN_WIDTH), cur(0)),
                  pl.BlockSpec((rows, KV_WIDTH), cur(kcol)),
                  pl.BlockSpec((WINDOW, KV_WIDTH), prev(kcol)),
                  pl.BlockSpec((rows, KV_WIDTH), cur(vcol)),
                  pl.BlockSpec((WINDOW, KV_WIDTH), prev(vcol))],
        out_specs=pl.BlockSpec((rows, ATTN_WIDTH), lambda b, i: (b * ns + i, 0)),
        out_shape=jax.ShapeDtypeStruct((N_PROMPT, ATTN_WIDTH), BF16),
        compiler_params=_params("arbitrary", "arbitrary"),
        name="attn_prompt",
    )(sinks, proj, proj, proj, proj, proj)


def _attn_sample_kernel(sink_ref, q_ref, kn_ref, vn_ref, kb_ref, vb_ref, o_ref, ko_ref, vo_ref):
    n_rows = GQA * DEC_SEQ
    tpos = lax.broadcasted_iota(jnp.int32, (n_rows, 2 * WINDOW), 0) % DEC_SEQ
    kidx = lax.broadcasted_iota(jnp.int32, (n_rows, 2 * WINDOW), 1)
    dist_i = jnp.where(kidx < WINDOW, tpos + WINDOW - kidx, tpos - (kidx - WINDOW))
    valid = (dist_i >= 0) & (dist_i <= WINDOW) & (kidx < WINDOW + DEC_SEQ)
    dist = dist_i.astype(F32)
    pad = jnp.zeros((WINDOW - DEC_SEQ, HEAD_DIM), F32)
    for bb in range(SAMPLE_BB):
        rows = slice(bb * DEC_SEQ, (bb + 1) * DEC_SEQ)
        for kvh in range(KV_HEADS):
            cols = slice(kvh * HEAD_DIM, (kvh + 1) * HEAD_DIM)
            kk = jnp.concatenate([kb_ref[bb, :, cols], kn_ref[rows, cols], pad], axis=0)
            vv = jnp.concatenate([vb_ref[bb, :, cols], vn_ref[rows, cols], pad], axis=0)
            heads = [kvh * GQA + g for g in range(GQA)]
            qs = jnp.concatenate([q_ref[rows, h * HEAD_DIM:(h + 1) * HEAD_DIM] for h in heads], axis=0)
            slope = _head_column([2.0 ** (-8.0 * (h + 1) / ATTN_HEADS) for h in heads], DEC_SEQ, n_rows)
            sink = _head_column([sink_ref[h] for h in heads], DEC_SEQ, n_rows)
            s = _mm_nt(qs, kk) * (HEAD_DIM ** -0.5) - slope * dist
            s = jnp.where(valid, s, NEG_INF)
            p, denom = _softmax_with_sink(s, sink)
            o = _mm(p, vv) / denom
            for g, h in enumerate(heads):
                o_ref[rows, h * HEAD_DIM:(h + 1) * HEAD_DIM] = o[g * DEC_SEQ:(g + 1) * DEC_SEQ]
        ko_ref[bb, 0:WINDOW - DEC_SEQ, :] = kb_ref[bb, DEC_SEQ:WINDOW, :]
        ko_ref[bb, WINDOW - DEC_SEQ:WINDOW, :] = kn_ref[rows, :]
        vo_ref[bb, 0:WINDOW - DEC_SEQ, :] = vb_ref[bb, DEC_SEQ:WINDOW, :]
        vo_ref[bb, WINDOW - DEC_SEQ:WINDOW, :] = vn_ref[rows, :]


def _attn_sample(proj, sinks, k_buf, v_buf, layer):
    rows = SAMPLE_BB * DEC_SEQ
    row0 = N_PROMPT // rows
    kcol, vcol = COL_K // KV_WIDTH, COL_V // KV_WIDTH
    cache_in = pl.BlockSpec((None, SAMPLE_BB, WINDOW, KV_WIDTH), lambda i: (layer, i, 0, 0))
    cache = pl.BlockSpec((SAMPLE_BB, WINDOW, KV_WIDTH), lambda i: (i, 0, 0))
    cache_shape = jax.ShapeDtypeStruct((DEC_BATCH, WINDOW, KV_WIDTH), F32)
    return pl.pallas_call(
        _attn_sample_kernel,
        grid=(DEC_BATCH // SAMPLE_BB,),
        in_specs=[pl.BlockSpec(memory_space=pltpu.SMEM),
                  pl.BlockSpec((rows, ATTN_WIDTH), lambda i: (row0 + i, 0)),
                  pl.BlockSpec((rows, KV_WIDTH), lambda i: (row0 + i, kcol)),
                  pl.BlockSpec((rows, KV_WIDTH), lambda i: (row0 + i, vcol)),
                  cache_in, cache_in],
        out_specs=[pl.BlockSpec((rows, ATTN_WIDTH), lambda i: (i, 0)), cache, cache],
        out_shape=[jax.ShapeDtypeStruct((N_SAMPLE, ATTN_WIDTH), F32), cache_shape, cache_shape],
        compiler_params=_params("arbitrary"),
        name="attn_sample",
    )(sinks, proj, proj, proj, k_buf, v_buf)


def _cumsum_rows(x):
    n = x.shape[0]
    row = lax.broadcasted_iota(jnp.int32, x.shape, 0)
    step = 1
    while step < n:
        x = x + jnp.where(row >= step, pltpu.roll(x, step, axis=0), 0.0)
        step *= 2
    return x


def _gdn_kernel(*refs, chunk, rows_in, has_state):
    if has_state:
        b1_ref, b2_ref, b3_ref, cw_ref, gp_ref, nw_ref, st_ref, s0_ref, o_ref, s_ref, ext_ref = refs
        first = True
    else:
        b1_ref, b2_ref, b3_ref, cw_ref, gp_ref, nw_ref, o_ref, s_ref, ext_ref = refs
        first = pl.program_id(1) == 0

    if has_state:
        ext_ref[0:8, :] = st_ref[...]
        s_ref[...] = s0_ref[...]
    else:
        @pl.when(first)
        def _():
            ext_ref[0:8, :] = jnp.zeros((8, CONV_CH), F32)
            s_ref[...] = jnp.zeros_like(s_ref)

        @pl.when(jnp.logical_not(first))
        def _():
            ext_ref[0:8, :] = ext_ref[chunk:chunk + 8, :]

    ext_ref[8:8 + rows_in, 0:GDN_COLS] = b1_ref[...]
    ext_ref[8:8 + rows_in, GDN_COLS:CONV_CH] = b2_ref[...]
    if rows_in < chunk:
        ext_ref[8 + rows_in:8 + chunk, :] = jnp.zeros((chunk - rows_in, CONV_CH), F32)

    lane = lax.broadcasted_iota(jnp.int32, (chunk, LANES), 1)
    row = lax.broadcasted_iota(jnp.int32, (chunk, LANES), 0)
    if rows_in < chunk:
        ab = jnp.concatenate([b3_ref[:, DN_VW:DN_VW + LANES],
                              jnp.zeros((chunk - rows_in, LANES), F32)], axis=0)
    else:
        ab = b3_ref[:, DN_VW:DN_VW + LANES]
    live = (lane < 2 * DN_HEADS) & (row < rows_in)
    ab = jnp.where(live, ab, 0.0)
    a_scale = gp_ref[0:1, :]
    dt_bias = gp_ref[1:2, :]
    g_all = jnp.where(live, a_scale * jax.nn.softplus(ab + dt_bias), 0.0)
    beta_all = jnp.where(live, jax.nn.sigmoid(ab), 0.0)
    gcum = _cumsum_rows(g_all)
    gcum_t = gcum.T
    g_last_all = gcum[chunk - 1:chunk, :]

    ri = lax.broadcasted_iota(jnp.int32, (chunk, chunk), 0)
    ci = lax.broadcasted_iota(jnp.int32, (chunk, chunk), 1)
    incl = ri >= ci
    strict = ri > ci

    def conv_strip(col0):
        cols = slice(col0, col0 + LANES)
        acc = ext_ref[8:8 + chunk, cols] * cw_ref[CONV_W - 1:CONV_W, cols]
        for tap in range(1, CONV_W):
            acc = acc + ext_ref[8 - tap:8 - tap + chunk, cols] * cw_ref[CONV_W - 1 - tap:CONV_W - tap, cols]
        return _silu(acc)

    def l2norm(x):
        return x * lax.rsqrt(jnp.sum(x * x, axis=-1, keepdims=True) + RMS_EPS)

    def run_heads(heads):
        q = {h: l2norm(conv_strip(h * DN_DK)) * (DN_DK ** -0.5) for h in heads}
        k = {h: l2norm(conv_strip(DN_KW + h * DN_DK)) for h in heads}
        v = {h: conv_strip(2 * DN_KW + h * DN_DV) for h in heads}
        gc_col = {h: gcum[:, h:h + 1] for h in heads}
        g_last = {h: g_last_all[:, h:h + 1] for h in heads}
        beta = {h: beta_all[:, DN_HEADS + h:DN_HEADS + h + 1] for h in heads}
        decay = {h: jnp.where(incl, jnp.exp(jnp.where(incl, gc_col[h] - gcum_t[h:h + 1, :], 0.0)), 0.0)
                 for h in heads}
        eg = {h: jnp.exp(gc_col[h]) for h in heads}
        kb = {h: k[h] * beta[h] for h in heads}
        kk = {h: _mm_nt(jnp.concatenate([kb[h], q[h]], axis=0), k[h]) for h in heads}
        qk = {h: jnp.where(incl, kk[h][chunk:] * decay[h], 0.0) for h in heads}
        neg_a = {h: jnp.where(strict, -kk[h][:chunk] * decay[h], 0.0) for h in heads}
        base = min(GDN_INV_BASE, chunk)
        p = {h: jnp.where(ri // base == ci // base, neg_a[h], 0.0) for h in heads}
        eye = (ri == ci).astype(F32)
        inv = {h: eye + p[h] for h in heads}
        span = 2
        while span < base:
            p = {h: _mm(p[h], p[h]) for h in heads}
            inv = {h: inv[h] + _mm(inv[h], p[h]) for h in heads}
            span *= 2
        size = base
        while size < chunk:
            lower_left = (ri // (2 * size) == ci // (2 * size)) & (ri // size != ci // size)
            off = {h: jnp.where(lower_left, neg_a[h], 0.0) for h in heads}
            inv = {h: inv[h] + _mm(_mm(inv[h], off[h]), inv[h]) for h in heads}
            size *= 2
        sol = {h: _mm(inv[h], jnp.concatenate([v[h] * beta[h], kb[h] * eg[h]], axis=1)) for h in heads}
        s_prev = {h: s_ref[h] for h in heads}
        ws = {h: _mm(jnp.concatenate([sol[h][:, DN_DV:], q[h] * eg[h]], axis=0), s_prev[h]) for h in heads}
        v_new = {h: sol[h][:, :DN_DV] - ws[h][:chunk] for h in heads}
        o = {h: ws[h][chunk:] + _mm(qk[h], v_new[h]) for h in heads}
        k_dec = {h: k[h] * jnp.exp(g_last[h] - gc_col[h]) for h in heads}
        for h in heads:
            s_ref[h] = s_prev[h] * jnp.exp(g_last[h]) + _mm_tn(k_dec[h], v_new[h])
        for h in heads:
            on = o[h] * lax.rsqrt(jnp.mean(o[h] * o[h], axis=-1, keepdims=True) + RMS_EPS)
            z = b3_ref[:, h * DN_DV:(h + 1) * DN_DV]
            out = on[0:rows_in] * nw_ref[...] * _silu(z)
            o_ref[:, h * DN_DV:(h + 1) * DN_DV] = out.astype(o_ref.dtype)

    for first_head in range(0, DN_HEADS, GDN_HEAD_GROUP):
        run_heads(range(first_head, first_head + GDN_HEAD_GROUP))


def _gdn_gate_params(a_log, dt_bias):
    rows = jnp.stack([-jnp.exp(a_log.astype(F32)), dt_bias.astype(F32)], axis=1)
    return jnp.pad(rows, ((0, 0), (0, 0), (0, LANES - DN_HEADS)))


def _gdn_prompt(proj, conv_w, gp, norm_w, layer):
    nc = SEQ // GDN_CHUNK
    c0 = COL_QKV // GDN_COLS

    def win(c):
        return pl.BlockSpec((GDN_CHUNK, GDN_COLS), lambda b, i: (b * nc + i, c))

    return pl.pallas_call(
        functools.partial(_gdn_kernel, chunk=GDN_CHUNK, rows_in=GDN_CHUNK, has_state=False),
        grid=(BATCH, nc),
        in_specs=[win(c0), win(c0 + 1), win(c0 + 2), _layer_slab((CONV_W, CONV_CH), layer),
                  _layer_slab((2, LANES), layer), _layer_slab((1, DN_DV), layer)],
        out_specs=[pl.BlockSpec((GDN_CHUNK, DN_VW), lambda b, i: (b * nc + i, 0)),
                   pl.BlockSpec((None, DN_HEADS, DN_DK, DN_DV), lambda b, i: (b, 0, 0, 0))],
        out_shape=[jax.ShapeDtypeStruct((N_PROMPT, DN_VW), BF16),
                   jax.ShapeDtypeStruct((BATCH, DN_HEADS, DN_DK, DN_DV), F32)],
        scratch_shapes=[pltpu.VMEM((8 + GDN_CHUNK, CONV_CH), F32)],
        compiler_params=_params("arbitrary", "arbitrary"),
        name="gdn_prompt",
    )(proj, proj, proj, conv_w, gp, norm_w)


def _gdn_sample(proj, conv_w, gp, norm_w, conv_state8, s0, chunk, layer):
    row0 = N_PROMPT // DEC_SEQ
    c0 = COL_QKV // GDN_COLS

    def win(c):
        return pl.BlockSpec((DEC_SEQ, GDN_COLS), lambda b: (row0 + b, c))

    state = pl.BlockSpec((None, DN_HEADS, DN_DK, DN_DV), lambda b: (b, 0, 0, 0))
    return pl.pallas_call(
        functools.partial(_gdn_kernel, chunk=chunk, rows_in=DEC_SEQ, has_state=True),
        grid=(DEC_BATCH,),
        in_specs=[win(c0), win(c0 + 1), win(c0 + 2), _layer_slab((CONV_W, CONV_CH), layer),
                  _layer_slab((2, LANES), layer), _layer_slab((1, DN_DV), layer),
                  pl.BlockSpec((None, None, 8, CONV_CH), lambda b: (layer, b, 0, 0)),
                  pl.BlockSpec((None, None, DN_HEADS, DN_DK, DN_DV), lambda b: (layer, b, 0, 0, 0))],
        out_specs=[pl.BlockSpec((DEC_SEQ, DN_VW), lambda b: (b, 0)), state],
        out_shape=[jax.ShapeDtypeStruct((N_SAMPLE, DN_VW), F32),
                   jax.ShapeDtypeStruct((DEC_BATCH, DN_HEADS, DN_DK, DN_DV), F32)],
        scratch_shapes=[pltpu.VMEM((8 + chunk, CONV_CH), F32)],
        compiler_params=_params("arbitrary"),
        name="gdn_sample",
    )(proj, proj, proj, conv_w, gp, norm_w, conv_state8, s0)


def _out_router_kernel(map_ref, mas_ref, mdp_ref, mds_ref, x_ref, wo_ref, g_ref, b_ref, wr_ref, br_ref,
                       x1_ref, eid_ref, gate_ref, ma_tail, md_tail):
    n_sub = OUT_TM // OUT_SUB
    tail = slice(OUT_TM - OUT_SUB, OUT_TM)
    is_last = pl.program_id(0) == pl.num_programs(0) - 1

    @pl.when(is_last)
    def _():
        ma_tail[...] = mas_ref[...].astype(BF16)
        md_tail[...] = mds_ref[...].astype(BF16)

    @pl.when(jnp.logical_not(is_last))
    def _():
        ma_tail[...] = map_ref[tail, :]
        md_tail[...] = mdp_ref[tail, :]

    def mix_of(c):
        rows = slice(c * OUT_SUB, (c + 1) * OUT_SUB)
        ma, md = (ma_tail[...], md_tail[...]) if c == n_sub - 1 else (map_ref[rows, :], mdp_ref[rows, :])
        return (jnp.dot(ma, wo_ref[0:ATTN_WIDTH, :], preferred_element_type=F32)
                + jnp.dot(md, wo_ref[ATTN_WIDTH:MIX_WIDTH, :], preferred_element_type=F32))

    mix = mix_of(0)
    for c in range(n_sub):
        nxt = mix_of(c + 1) if c + 1 < n_sub else None
        rows = slice(c * OUT_SUB, (c + 1) * OUT_SUB)
        x1 = _layer_norm(DEEPNORM_ALPHA * x_ref[rows, :] + mix, g_ref[...], b_ref[...])
        x1_ref[rows, :] = x1
        logits = jnp.dot(x1.astype(BF16), wr_ref[...], preferred_element_type=F32) + br_ref[...]
        eid, gate = _route(logits)
        eid_ref[rows, :] = eid
        gate_ref[rows, :] = gate
        mix = nxt


def _route(logits):
    lane = lax.broadcasted_iota(jnp.int32, logits.shape, 1)
    lane_f = lane.astype(F32)
    far = float(LANES)
    is_grp = lane < N_GROUPS
    grp_max = jnp.max(jnp.where(is_grp, logits, NEG_INF), axis=-1, keepdims=True)
    grp = jnp.min(jnp.where(is_grp & (logits == grp_max), lane_f, far), axis=-1, keepdims=True)
    grp_gate = 1.0 / jnp.sum(jnp.where(is_grp, jnp.exp(logits - grp_max), 0.0), axis=-1, keepdims=True)
    lo = N_GROUPS + EXPERTS_PER_GROUP * grp
    in_grp = (lane_f >= lo) & (lane_f < lo + EXPERTS_PER_GROUP)
    v1 = jnp.max(jnp.where(in_grp, logits, NEG_INF), axis=-1, keepdims=True)
    i1 = jnp.min(jnp.where(in_grp & (logits == v1), lane_f, far), axis=-1, keepdims=True)
    rest = in_grp & (lane_f != i1)
    v2 = jnp.max(jnp.where(rest, logits, NEG_INF), axis=-1, keepdims=True)
    i2 = jnp.min(jnp.where(rest & (logits == v2), lane_f, far), axis=-1, keepdims=True)
    t = jnp.exp(v2 - v1)
    g1 = grp_gate / (1.0 + t)
    g2 = g1 * t
    e1 = (i1 - N_GROUPS).astype(jnp.int32)
    e2 = (i2 - N_GROUPS).astype(jnp.int32)
    return (jnp.where(lane == 0, e1, jnp.where(lane == 1, e2, 0)),
            jnp.where(lane == 0, g1, jnp.where(lane == 1, g2, 0.0)))


def _out_router(mix_a_p, mix_a_s, mix_d_p, mix_d_s, x, wo_b, g, b, wr_b, br, layer):
    assert N_SAMPLE == OUT_SUB and N_PROMPT % OUT_TM == OUT_TM - OUT_SUB and ATTN_WIDTH == DN_VW

    def row(w):
        return pl.BlockSpec((OUT_TM, w), lambda i: (i, 0))

    prompt = pl.BlockSpec((OUT_TM, ATTN_WIDTH), lambda i: (i, 0))
    sample = pl.BlockSpec((N_SAMPLE, ATTN_WIDTH), lambda i: (0, 0))
    return pl.pallas_call(
        _out_router_kernel,
        grid=(N_TOK // OUT_TM,),
        in_specs=[prompt, sample, prompt, sample, row(D_MODEL), _layer_slab((MIX_WIDTH, D_MODEL), layer),
                  _layer_slab((1, D_MODEL), layer), _layer_slab((1, D_MODEL), layer),
                  _layer_slab((D_MODEL, LANES), layer), _layer_slab((1, LANES), layer)],
        out_specs=[row(D_MODEL), row(LANES), row(LANES)],
        out_shape=[jax.ShapeDtypeStruct((N_TOK, D_MODEL), F32),
                   jax.ShapeDtypeStruct((N_TOK, LANES), jnp.int32),
                   jax.ShapeDtypeStruct((N_TOK, LANES), F32)],
        scratch_shapes=[pltpu.VMEM((OUT_SUB, ATTN_WIDTH), BF16), pltpu.VMEM((OUT_SUB, DN_VW), BF16)],
        compiler_params=_params("arbitrary", vmem_limit=VMEM_LIMIT_BIG),
        name="out_router",
    )(mix_a_p, mix_a_s, mix_d_p, mix_d_s, x, wo_b, g, b, wr_b, br)


def _plan_kernel(eid_ref, pos_ref, cnt_ref, run_ref, off_ref):
    phase = pl.program_id(0)
    i = pl.program_id(1)
    lane = lax.broadcasted_iota(jnp.int32, (ROW_TILE, LANES), 1)
    eid = eid_ref[...]
    hot0 = (lane == eid[:, 0:1]).astype(F32)
    hot1 = (lane == eid[:, 1:2]).astype(F32)
    hot = hot0 + hot1
    tile_cnt = jnp.sum(hot, axis=0, keepdims=True)

    @pl.when((phase == 0) & (i == 0))
    def _():
        cnt_ref[...] = jnp.zeros_like(cnt_ref)

    @pl.when(phase == 0)
    def _():
        cnt_ref[...] += tile_cnt

    @pl.when((phase == 1) & (i == 0))
    def _():
        cnt = cnt_ref[...]
        hi = jnp.floor(cnt * (1.0 / 256.0))
        lo = cnt - 256.0 * hi
        r = lax.broadcasted_iota(jnp.int32, (LANES, LANES), 0)
        c = lax.broadcasted_iota(jnp.int32, (LANES, LANES), 1)
        upper = (r < c).astype(F32)
        hi8 = jnp.broadcast_to(hi, (8, LANES))
        lo8 = jnp.broadcast_to(lo, (8, LANES))
        off = 256.0 * _mm(hi8, upper) + _mm(lo8, upper)
        off_ref[...] = off[0:1]
        run_ref[...] = jnp.zeros_like(run_ref)

    @pl.when(phase == 1)
    def _():
        r = lax.broadcasted_iota(jnp.int32, (ROW_TILE, ROW_TILE), 0)
        c = lax.broadcasted_iota(jnp.int32, (ROW_TILE, ROW_TILE), 1)
        before = (r > c).astype(F32)
        slot = _mm(before, hot) + run_ref[...] + off_ref[...]
        p0 = jnp.sum(hot0 * slot, axis=-1, keepdims=True).astype(jnp.int32)
        p1 = jnp.sum(hot1 * slot, axis=-1, keepdims=True).astype(jnp.int32)
        pos_ref[...] = jnp.where(lane == 0, p0, jnp.where(lane == 1, p1, 0))
        run_ref[...] += tile_cnt


def _plan(eid):
    return pl.pallas_call(
        _plan_kernel,
        grid=(2, N_TOK // ROW_TILE),
        in_specs=[pl.BlockSpec((ROW_TILE, LANES), lambda p, i: (i, 0))],
        out_specs=[pl.BlockSpec((ROW_TILE, LANES), lambda p, i: (i * p, 0)),
                   pl.BlockSpec((1, LANES), lambda p, i: (0, 0))],
        out_shape=[jax.ShapeDtypeStruct((N_TOK, LANES), jnp.int32),
                   jax.ShapeDtypeStruct((1, LANES), F32)],
        scratch_shapes=[pltpu.VMEM((1, LANES), F32), pltpu.VMEM((1, LANES), F32)],
        compiler_params=_params("arbitrary", "arbitrary"),
        name="moe_plan",
    )(eid)


def _work_items(counts):
    counts = counts.astype(jnp.int32)
    ends = jnp.cumsum(counts)
    starts = ends - counts
    first_tile = starts // MOE_TM
    n_tiles = jnp.where(counts > 0, (ends - 1) // MOE_TM - first_tile + 1, 0)
    item_end = jnp.cumsum(n_tiles)
    item_start = item_end - n_tiles
    n_items = item_end[-1]
    w = jnp.minimum(jnp.arange(N_MOE_ITEMS, dtype=jnp.int32), n_items - 1)
    expert = jnp.sum((item_end[None, :] <= w[:, None]).astype(jnp.int32), axis=1)
    tile = first_tile[expert] + (w - item_start[expert])
    prev_tile = jnp.concatenate([jnp.full((1,), -1, jnp.int32), tile[:-1]])
    prev_expert = jnp.concatenate([jnp.full((1,), -1, jnp.int32), expert[:-1]])
    valid = (jnp.arange(N_MOE_ITEMS, dtype=jnp.int32) < n_items).astype(jnp.int32)
    return (tile, expert, (tile != prev_tile).astype(jnp.int32), (expert != prev_expert).astype(jnp.int32),
            valid, starts[expert], ends[expert])


def _slot_tokens_kernel(pos_ref, tok_ref):
    group = 16

    def place(g, carry):
        a0 = g * group
        slots = [pos_ref[a0 + j] for j in range(group)]
        for j in range(group):
            tok_ref[slots[j]] = g * (group // 2) + j // 2
        return carry

    lax.fori_loop(0, N_ASSIGN // group, place, 0)


def _slot_tokens(pos_flat):
    return pl.pallas_call(
        _slot_tokens_kernel,
        grid_spec=pltpu.PrefetchScalarGridSpec(
            num_scalar_prefetch=1,
            grid=(1,),
            in_specs=[],
            out_specs=pl.BlockSpec(memory_space=pltpu.SMEM)),
        out_shape=jax.ShapeDtypeStruct((N_ASSIGN,), jnp.int32),
        compiler_params=_params("arbitrary"),
        name="moe_slot_tokens",
    )(pos_flat)


def _experts_kernel(tile_ref, exp_ref, first_ref, newexp_ref, valid_ref, gstart_ref, gend_ref, tok_ref,
                    x1_ref, wgu_ref, wdn_ref, o_ref, wgu_b, wdn_b, xs_ref, sem):
    w = pl.program_id(0)

    def start_gather(tile, into):
        base = tile * MOE_TM

        def issue(pair, carry):
            for priority in range(2):
                r = 2 * pair + priority
                pltpu.make_async_copy(x1_ref.at[pl.ds(tok_ref[base + r], 1), :],
                                      xs_ref.at[into, pl.ds(r, 1), :], sem.at[into]).start(priority=priority)
            return carry

        lax.fori_loop(0, MOE_TM // 2, issue, 0, unroll=4)

    @pl.when(w == 0)
    def _():
        start_gather(0, 0)

    @pl.when(valid_ref[w] == 1)
    def _():
        tile = tile_ref[w]
        buf = tile % 2

        @pl.when(first_ref[w] == 1)
        def _():
            pltpu.make_async_copy(x1_ref.at[pl.ds(0, MOE_TM), :], xs_ref.at[buf], sem.at[buf]).wait()

            @pl.when(tile + 1 < N_MOE_TILES)
            def _():
                start_gather(tile + 1, 1 - buf)

        @pl.when(newexp_ref[w] == 1)
        def _():
            wgu_b[...] = wgu_ref[...].astype(BF16)
            wdn_b[...] = wdn_ref[...].astype(BF16)

        h = jnp.dot(xs_ref[buf].astype(BF16), wgu_b[...], preferred_element_type=F32)
        act = _silu(h[:, :D_FF_EXPERT]) * h[:, D_FF_EXPERT:]
        y = jnp.dot(act.astype(BF16), wdn_b[...], preferred_element_type=F32)
        rows = tile_ref[w] * MOE_TM + lax.broadcasted_iota(jnp.int32, (MOE_TM, 1), 0)
        mine = (rows >= gstart_ref[w]) & (rows < gend_ref[w])

        @pl.when(first_ref[w] == 1)
        def _():
            o_ref[...] = jnp.where(mine, y, 0.0)

        @pl.when(first_ref[w] == 0)
        def _():
            o_ref[...] = jnp.where(mine, y, o_ref[...])


def _experts(items, slot_tokens, x1, w_gu, w_dn, layer):
    return pl.pallas_call(
        _experts_kernel,
        grid_spec=pltpu.PrefetchScalarGridSpec(
            num_scalar_prefetch=8,
            grid=(N_MOE_ITEMS,),
            in_specs=[pl.BlockSpec(memory_space=pl.ANY),
                      pl.BlockSpec((None, None, D_MODEL, 2 * D_FF_EXPERT),
                                   lambda w, t, e, *_: (layer, e[w], 0, 0)),
                      pl.BlockSpec((None, None, D_FF_EXPERT, D_MODEL),
                                   lambda w, t, e, *_: (layer, e[w], 0, 0))],
            out_specs=pl.BlockSpec((MOE_TM, D_MODEL), lambda w, t, e, *_: (t[w], 0)),
            scratch_shapes=[pltpu.VMEM((D_MODEL, 2 * D_FF_EXPERT), BF16),
                            pltpu.VMEM((D_FF_EXPERT, D_MODEL), BF16),
                            pltpu.VMEM((2, MOE_TM, D_MODEL), F32),
                            pltpu.SemaphoreType.DMA((2,))]),
        out_shape=jax.ShapeDtypeStruct((N_ASSIGN, D_MODEL), F32),
        compiler_params=_params("arbitrary"),
        name="moe_experts",
    )(*items, slot_tokens, x1, w_gu, w_dn)


def _combine_kernel(pos_ref, ys_ref, x1_ref, gate_ref, g_ref, b_ref, out_a_ref, out_b_ref, rows_ref, sem, *,
                    last_layer):
    i = pl.program_id(0)
    buf = i % 2

    def start_gather(tile, into):
        base = tile * (2 * ROW_TILE)

        def issue(r, carry):
            for pick in range(2):
                pltpu.make_async_copy(ys_ref.at[pl.ds(pos_ref[base + 2 * r + pick], 1), :],
                                      rows_ref.at[into, pick, pl.ds(r, 1), :], sem.at[into]).start(priority=pick)
            return carry

        lax.fori_loop(0, ROW_TILE, issue, 0, unroll=8)

    @pl.when(i == 0)
    def _():
        start_gather(0, 0)

    @pl.when(i + 1 < pl.num_programs(0))
    def _():
        start_gather(i + 1, 1 - buf)

    for pick in range(2):
        pltpu.make_async_copy(ys_ref.at[pl.ds(0, ROW_TILE), :], rows_ref.at[buf, pick], sem.at[buf]).wait()
    gate = gate_ref[...]
    y = gate[:, 0:1] * rows_ref[buf, 0] + gate[:, 1:2] * rows_ref[buf, 1]
    x2 = _layer_norm(DEEPNORM_ALPHA * x1_ref[...] + y, g_ref[...], b_ref[...])
    if last_layer:
        is_prompt = i < N_PROMPT // ROW_TILE

        @pl.when(is_prompt)
        def _():
            out_a_ref[...] = x2

        @pl.when(jnp.logical_not(is_prompt))
        def _():
            out_b_ref[...] = x2
    else:
        out_a_ref[...] = x2
        out_b_ref[...] = x2.astype(BF16)


def _combine(pos_flat, ys, x1, gates, g, b, layer):
    row = lambda w: pl.BlockSpec((ROW_TILE, w), lambda i, pos: (i, 0))
    vec = _layer_slab((1, D_MODEL), layer)
    last_layer = layer == DEPTH - 1
    if last_layer:
        assert N_SAMPLE == ROW_TILE
        last_prompt = N_PROMPT // ROW_TILE - 1
        out_specs = [pl.BlockSpec((ROW_TILE, D_MODEL), lambda i, pos: (jnp.minimum(i, last_prompt), 0)),
                     pl.BlockSpec((ROW_TILE, D_MODEL), lambda i, pos: (0, 0))]
        out_shape = [jax.ShapeDtypeStruct((N_PROMPT, D_MODEL), F32), jax.ShapeDtypeStruct((N_SAMPLE, D_MODEL), F32)]
    else:
        out_specs = [row(D_MODEL), row(D_MODEL)]
        out_shape = [jax.ShapeDtypeStruct((N_TOK, D_MODEL), F32), jax.ShapeDtypeStruct((N_TOK, D_MODEL), BF16)]
    return pl.pallas_call(
        functools.partial(_combine_kernel, last_layer=last_layer),
        grid_spec=pltpu.PrefetchScalarGridSpec(
            num_scalar_prefetch=1,
            grid=(N_TOK // ROW_TILE,),
            in_specs=[pl.BlockSpec(memory_space=pl.ANY), row(D_MODEL), row(LANES), vec, vec],
            out_specs=out_specs,
            scratch_shapes=[pltpu.VMEM((2, 2, ROW_TILE, D_MODEL), F32),
                            pltpu.SemaphoreType.DMA((2,))]),
        out_shape=out_shape,
        compiler_params=_params("arbitrary"),
        name="moe_combine",
    )(pos_flat, ys, x1, gates, g, b)


def _router_weights(wg, bg, we, be):
    pad = LANES - N_GROUPS - N_EXPERTS
    wr = jnp.concatenate([wg, we, jnp.zeros((DEPTH, D_MODEL, pad), wg.dtype)], axis=-1)
    br = jnp.concatenate([bg, be, jnp.zeros((DEPTH, pad), bg.dtype)], axis=-1)
    return wr.astype(BF16), br.astype(F32).reshape(DEPTH, 1, LANES)


def kernel(x_prompt, x_sample, cache_swa_k, cache_swa_v, state_conv, state_delta, ln_in_g, ln_in_b, w_in, conv_w,
           a_log, dt_bias, dn_norm_w, attn_sinks, w_out, ln1_g, ln1_b, router_group_w, router_group_b,
           router_expert_w, router_expert_b, w_gate_up, w_down, ln2_g, ln2_b):
    x, xb = _ln_in(x_prompt.reshape(N_PROMPT, D_MODEL), x_sample.reshape(N_SAMPLE, D_MODEL), ln_in_g, ln_in_b)
    k_buf = cache_swa_k.reshape(DEPTH, DEC_BATCH, WINDOW, KV_WIDTH)
    v_buf = cache_swa_v.reshape(DEPTH, DEC_BATCH, WINDOW, KV_WIDTH)
    conv_state8 = jnp.pad(state_conv, ((0, 0), (0, 0), (8 - (CONV_W - 1), 0), (0, 0)))
    gp = _gdn_gate_params(a_log, dt_bias)
    norm_w = dn_norm_w.reshape(DEPTH, 1, DN_DV).astype(F32)
    wo_b = w_out.astype(BF16)
    w_in_t = jnp.swapaxes(w_in, 1, 2)
    wr_b, br = _router_weights(router_group_w, router_group_b, router_expert_w, router_expert_b)
    ln1_g, ln1_b, ln2_g, ln2_b = (a.reshape(DEPTH, 1, D_MODEL) for a in (ln1_g, ln1_b, ln2_g, ln2_b))
    kp, vp, cp, sp = [], [], [], []
    kq, vq, cq, sq = [], [], [], []
    for l in range(DEPTH):
        proj = _proj(xb, w_in_t, l)
        sinks = attn_sinks[l].astype(F32)
        o_attn_p = _attn_prompt(proj, sinks)
        o_attn_s, k_s, v_s = _attn_sample(proj, sinks, k_buf, v_buf, l)
        o_dn_p, s_p = _gdn_prompt(proj, conv_w, gp, norm_w, l)
        o_dn_s, s_s = _gdn_sample(proj, conv_w, gp, norm_w, conv_state8, state_delta, GDN_CHUNK_SAMPLE, l)
        x1, eid, gates = _out_router(o_attn_p, o_attn_s, o_dn_p, o_dn_s, x, wo_b, ln1_g, ln1_b, wr_b, br, l)
        pos, counts = _plan(eid)
        pos_flat = pos[:, :2].reshape(N_ASSIGN)
        items = _work_items(counts[0, :N_EXPERTS])
        ys = _experts(items, _slot_tokens(pos_flat), x1, w_gate_up, w_down, l)
        x, xb = _combine(pos_flat, ys, x1, gates, ln2_g, ln2_b, l)

        def prompt_tail(rows, c0, c1):
            return jnp.stack([proj[(b + 1) * SEQ - rows:(b + 1) * SEQ, c0:c1] for b in range(BATCH)])

        kp.append(prompt_tail(WINDOW, COL_K, COL_V).reshape(BATCH, WINDOW, KV_HEADS, HEAD_DIM))
        vp.append(prompt_tail(WINDOW, COL_V, COL_QKV).reshape(BATCH, WINDOW, KV_HEADS, HEAD_DIM))
        cp.append(prompt_tail(CONV_W - 1, COL_QKV, COL_Z))
        sp.append(s_p)
        kq.append(k_s.reshape(DEC_BATCH, WINDOW, KV_HEADS, HEAD_DIM))
        vq.append(v_s.reshape(DEC_BATCH, WINDOW, KV_HEADS, HEAD_DIM))
        cq.append(proj[N_PROMPT:, COL_QKV:COL_Z].reshape(DEC_BATCH, DEC_SEQ, CONV_CH)[:, DEC_SEQ - (CONV_W - 1):])
        sq.append(s_s)
    return (x.reshape(BATCH, SEQ, D_MODEL), xb.reshape(DEC_BATCH, DEC_SEQ, D_MODEL),
            jnp.stack(kp), jnp.stack(vp), jnp.stack(cp), jnp.stack(sp),
            jnp.stack(kq), jnp.stack(vq), jnp.stack(cq), jnp.stack(sq))
```

```python
import functools

import jax
import jax.numpy as jnp
from jax import lax
from jax.experimental import pallas as pl
from jax.experimental.pallas import tpu as pltpu

D_MODEL = 2048
BATCH = 4
SEQ = 2048
DEPTH = 4
DEC_BATCH = 32
DEC_SEQ = 8
HEAD_DIM = 128
ATTN_HEADS = 8
KV_HEADS = 2
GQA = ATTN_HEADS // KV_HEADS
ATTN_WIDTH = ATTN_HEADS * HEAD_DIM
KV_WIDTH = KV_HEADS * HEAD_DIM
WINDOW = 128
DN_HEADS = 8
DN_DK = 128
DN_DV = 128
DN_KW = DN_HEADS * DN_DK
DN_VW = DN_HEADS * DN_DV
CONV_W = 4
CONV_CH = 2 * DN_KW + DN_VW
MIX_WIDTH = ATTN_WIDTH + DN_VW
IN_COLS = ATTN_WIDTH + 2 * KV_WIDTH + CONV_CH + DN_VW + 2 * DN_HEADS
N_GROUPS = 4
EXPERTS_PER_GROUP = 8
N_EXPERTS = N_GROUPS * EXPERTS_PER_GROUP
D_FF_EXPERT = 256
DEEPNORM_ALPHA = (2 * DEPTH) ** 0.25
LN_EPS = 1e-5
RMS_EPS = 1e-6
NEG_INF = -1e30

N_PROMPT = BATCH * SEQ
N_SAMPLE = DEC_BATCH * DEC_SEQ
N_TOK = N_PROMPT + N_SAMPLE
N_ASSIGN = 2 * N_TOK

COL_K = ATTN_WIDTH
COL_V = COL_K + KV_WIDTH
COL_QKV = COL_V + KV_WIDTH
COL_Z = COL_QKV + CONV_CH
COL_AB = COL_Z + DN_VW

LANES = 128
SUBLANES = 8
ROW_TILE = 256
PROJ_TM = 1408
PROJ_TN = 1024
OUT_TM = 768
OUT_SUB = 256
GDN_CHUNK = 128
GDN_CHUNK_SAMPLE = 16
GDN_HEAD_GROUP = 8
GDN_INV_BASE = 16
GDN_COLS = 1536
SAMPLE_BB = 4
ATTN_QB = 4
MOE_TM = 256
N_MOE_TILES = N_ASSIGN // MOE_TM
N_MOE_ITEMS = N_MOE_TILES + N_EXPERTS - 1
VMEM_LIMIT = 48 * 1024 * 1024
VMEM_LIMIT_BIG = 56 * 1024 * 1024

F32 = jnp.float32
BF16 = jnp.bfloat16


def _params(*sem, vmem_limit=VMEM_LIMIT):
    return pltpu.CompilerParams(dimension_semantics=sem, vmem_limit_bytes=vmem_limit)


def _mm(a, b):
    return jnp.dot(a.astype(BF16), b.astype(BF16), preferred_element_type=F32)


def _mm_nt(a, b):
    return lax.dot_general(a.astype(BF16), b.astype(BF16), (((1,), (1,)), ((), ())),
                           preferred_element_type=F32)


def _mm_tn(a, b):
    return lax.dot_general(a.astype(BF16), b.astype(BF16), (((0,), (0,)), ((), ())),
                           preferred_element_type=F32)


def _layer_slab(shape, layer):
    return pl.BlockSpec((None,) + tuple(shape), lambda *_: (layer,) + (0,) * len(shape),
                        pipeline_mode=pl.Buffered(1))


def _layer_norm(x, g, b):
    mu = jnp.mean(x, axis=-1, keepdims=True)
    xc = x - mu
    var = jnp.mean(xc * xc, axis=-1, keepdims=True)
    return xc * lax.rsqrt(var + LN_EPS) * g + b


def _silu(x):
    return x / (1.0 + jnp.exp(-x))


def _ln_in_kernel(xp_ref, xs_ref, g_ref, b_ref, o_ref, ob_ref):
    def emit(x_ref):
        y = _layer_norm(x_ref[...], g_ref[...], b_ref[...])
        o_ref[...] = y
        ob_ref[...] = y.astype(BF16)

    is_prompt = pl.program_id(0) < N_PROMPT // ROW_TILE
    pl.when(is_prompt)(lambda: emit(xp_ref))
    pl.when(jnp.logical_not(is_prompt))(lambda: emit(xs_ref))


def _ln_in(xp, xs, g, b):
    assert N_SAMPLE == ROW_TILE
    row = pl.BlockSpec((ROW_TILE, D_MODEL), lambda i: (i, 0))
    vec = pl.BlockSpec((1, D_MODEL), lambda i: (0, 0))
    last_prompt = N_PROMPT // ROW_TILE - 1
    return pl.pallas_call(
        _ln_in_kernel,
        grid=(N_TOK // ROW_TILE,),
        in_specs=[pl.BlockSpec((ROW_TILE, D_MODEL), lambda i: (jnp.minimum(i, last_prompt), 0)),
                  pl.BlockSpec((ROW_TILE, D_MODEL), lambda i: (0, 0)), vec, vec],
        out_specs=[row, row],
        out_shape=[jax.ShapeDtypeStruct((N_TOK, D_MODEL), F32),
                   jax.ShapeDtypeStruct((N_TOK, D_MODEL), BF16)],
        compiler_params=_params("arbitrary"),
        name="ln_in",
    )(xp, xs, g.reshape(1, D_MODEL), b.reshape(1, D_MODEL))


def _proj_kernel(x_ref, wt_ref, o_ref, wb_ref):
    @pl.when(pl.program_id(1) == 0)
    def _():
        wb_ref[...] = wt_ref[...].astype(BF16)

    o_ref[...] = lax.dot_general(x_ref[...], wb_ref[...], (((1,), (1,)), ((), ())), preferred_element_type=F32)


def _proj(xb, w_in_t, layer):
    return pl.pallas_call(
        _proj_kernel,
        grid=(pl.cdiv(IN_COLS, PROJ_TN), N_TOK // PROJ_TM),
        in_specs=[pl.BlockSpec((PROJ_TM, D_MODEL), lambda j, i: (i, 0)),
                  pl.BlockSpec((None, PROJ_TN, D_MODEL), lambda j, i: (layer, j, 0))],
        out_specs=pl.BlockSpec((PROJ_TM, PROJ_TN), lambda j, i: (i, j)),
        out_shape=jax.ShapeDtypeStruct((N_TOK, IN_COLS), F32),
        scratch_shapes=[pltpu.VMEM((PROJ_TN, D_MODEL), BF16)],
        compiler_params=_params("arbitrary", "arbitrary", vmem_limit=VMEM_LIMIT_BIG),
        name="in_proj",
    )(xb, w_in_t)


def _head_column(values, rows_per_head, n_rows):
    grp = lax.broadcasted_iota(jnp.int32, (n_rows, 1), 0) // rows_per_head
    col = jnp.full((n_rows, 1), values[-1], F32)
    for g in range(len(values) - 2, -1, -1):
        col = jnp.where(grp == g, values[g], col)
    return col


def _softmax_with_sink(s, sink_col):
    m = jnp.maximum(jnp.max(s, axis=-1, keepdims=True), sink_col)
    p = jnp.exp(s - m)
    denom = jnp.sum(p, axis=-1, keepdims=True) + jnp.exp(sink_col - m)
    return p, denom


def _attn_prompt_kernel(sink_ref, q_ref, kc_ref, kp_ref, vc_ref, vp_ref, o_ref):
    step = pl.program_id(1)
    n_rows = GQA * WINDOW
    qpos = lax.broadcasted_iota(jnp.int32, (n_rows, 2 * WINDOW), 0) % WINDOW
    kidx = lax.broadcasted_iota(jnp.int32, (n_rows, 2 * WINDOW), 1)
    dist_i = qpos + WINDOW - kidx
    in_window = (dist_i >= 0) & (dist_i <= WINDOW)
    dist = dist_i.astype(F32)
    for sub in range(ATTN_QB):
        rows = slice(sub * WINDOW, (sub + 1) * WINDOW)
        before = slice((sub - 1) * WINDOW, sub * WINDOW)
        valid = in_window if sub > 0 else in_window & ((kidx >= WINDOW) | (step > 0))
        for kvh in range(KV_HEADS):
            cols = slice(kvh * HEAD_DIM, (kvh + 1) * HEAD_DIM)
            k_before = kc_ref[before, cols] if sub > 0 else kp_ref[:, cols]
            v_before = vc_ref[before, cols] if sub > 0 else vp_ref[:, cols]
            kk = jnp.concatenate([k_before, kc_ref[rows, cols]], axis=0)
            vv = jnp.concatenate([v_before, vc_ref[rows, cols]], axis=0)
            heads = [kvh * GQA + g for g in range(GQA)]
            qs = jnp.concatenate([q_ref[rows, h * HEAD_DIM:(h + 1) * HEAD_DIM] for h in heads], axis=0)
            slope = _head_column([2.0 ** (-8.0 * (h + 1) / ATTN_HEADS) for h in heads], WINDOW, n_rows)
            sink = _head_column([sink_ref[h] for h in heads], WINDOW, n_rows)
            s = _mm_nt(qs, kk) * (HEAD_DIM ** -0.5) - slope * dist
            s = jnp.where(valid, s, NEG_INF)
            p, denom = _softmax_with_sink(s, sink)
            o = _mm(p, vv) / denom
            for g, h in enumerate(heads):
                o_ref[rows, h * HEAD_DIM:(h + 1) * HEAD_DIM] = o[g * WINDOW:(g + 1) * WINDOW].astype(BF16)


def _attn_prompt(proj, sinks):
    nb = SEQ // WINDOW
    ns = nb // ATTN_QB
    rows = ATTN_QB * WINDOW
    kcol, vcol = COL_K // KV_WIDTH, COL_V // KV_WIDTH

    def cur(c):
        return lambda b, i: (b * ns + i, c)

    def prev(c):
        return lambda b, i: (jnp.maximum(b * nb + i * ATTN_QB - 1, 0), c)

    return pl.pallas_call(
        _attn_prompt_kernel,
        grid=(BATCH, ns),
        in_specs=[pl.BlockSpec(memory_space=pltpu.SMEM),
                  pl.BlockSpec((rows, ATTN_WIDTH), cur(0)),
                  pl.BlockSpec((rows, KV_WIDTH), cur(kcol)),
                  pl.BlockSpec((WINDOW, KV_WIDTH), prev(kcol)),
                  pl.BlockSpec((rows, KV_WIDTH), cur(vcol)),
                  pl.BlockSpec((WINDOW, KV_WIDTH), prev(vcol))],
        out_specs=pl.BlockSpec((rows, ATTN_WIDTH), lambda b, i: (b * ns + i, 0)),
        out_shape=jax.ShapeDtypeStruct((N_PROMPT, ATTN_WIDTH), BF16),
        compiler_params=_params("arbitrary", "arbitrary"),
        name="attn_prompt",
    )(sinks, proj, proj, proj, proj, proj)


def _attn_sample_kernel(sink_ref, q_ref, kn_ref, vn_ref, kb_ref, vb_ref, o_ref, ko_ref, vo_ref):
    n_rows = GQA * DEC_SEQ
    tpos = lax.broadcasted_iota(jnp.int32, (n_rows, 2 * WINDOW), 0) % DEC_SEQ
    kidx = lax.broadcasted_iota(jnp.int32, (n_rows, 2 * WINDOW), 1)
    dist_i = jnp.where(kidx < WINDOW, tpos + WINDOW - kidx, tpos - (kidx - WINDOW))
    valid = (dist_i >= 0) & (dist_i <= WINDOW) & (kidx < WINDOW + DEC_SEQ)
    dist = dist_i.astype(F32)
    pad = jnp.zeros((WINDOW - DEC_SEQ, HEAD_DIM), F32)
    for bb in range(SAMPLE_BB):
        rows = slice(bb * DEC_SEQ, (bb + 1) * DEC_SEQ)
        for kvh in range(KV_HEADS):
            cols = slice(kvh * HEAD_DIM, (kvh + 1) * HEAD_DIM)
            kk = jnp.concatenate([kb_ref[bb, :, cols], kn_ref[rows, cols], pad], axis=0)
            vv = jnp.concatenate([vb_ref[bb, :, cols], vn_ref[rows, cols], pad], axis=0)
            heads = [kvh * GQA + g for g in range(GQA)]
            qs = jnp.concatenate([q_ref[rows, h * HEAD_DIM:(h + 1) * HEAD_DIM] for h in heads], axis=0)
            slope = _head_column([2.0 ** (-8.0 * (h + 1) / ATTN_HEADS) for h in heads], DEC_SEQ, n_rows)
            sink = _head_column([sink_ref[h] for h in heads], DEC_SEQ, n_rows)
            s = _mm_nt(qs, kk) * (HEAD_DIM ** -0.5) - slope * dist
            s = jnp.where(valid, s, NEG_INF)
            p, denom = _softmax_with_sink(s, sink)
            o = _mm(p, vv) / denom
            for g, h in enumerate(heads):
                o_ref[rows, h * HEAD_DIM:(h + 1) * HEAD_DIM] = o[g * DEC_SEQ:(g + 1) * DEC_SEQ]
        ko_ref[bb, 0:WINDOW - DEC_SEQ, :] = kb_ref[bb, DEC_SEQ:WINDOW, :]
        ko_ref[bb, WINDOW - DEC_SEQ:WINDOW, :] = kn_ref[rows, :]
        vo_ref[bb, 0:WINDOW - DEC_SEQ, :] = vb_ref[bb, DEC_SEQ:WINDOW, :]
        vo_ref[bb, WINDOW - DEC_SEQ:WINDOW, :] = vn_ref[rows, :]


def _attn_sample(proj, sinks, k_buf, v_buf, layer):
    rows = SAMPLE_BB * DEC_SEQ
    row0 = N_PROMPT // rows
    kcol, vcol = COL_K // KV_WIDTH, COL_V // KV_WIDTH
    cache_in = pl.BlockSpec((None, SAMPLE_BB, WINDOW, KV_WIDTH), lambda i: (layer, i, 0, 0))
    cache = pl.BlockSpec((SAMPLE_BB, WINDOW, KV_WIDTH), lambda i: (i, 0, 0))
    cache_shape = jax.ShapeDtypeStruct((DEC_BATCH, WINDOW, KV_WIDTH), F32)
    return pl.pallas_call(
        _attn_sample_kernel,
        grid=(DEC_BATCH // SAMPLE_BB,),
        in_specs=[pl.BlockSpec(memory_space=pltpu.SMEM),
                  pl.BlockSpec((rows, ATTN_WIDTH), lambda i: (row0 + i, 0)),
                  pl.BlockSpec((rows, KV_WIDTH), lambda i: (row0 + i, kcol)),
                  pl.BlockSpec((rows, KV_WIDTH), lambda i: (row0 + i, vcol)),
                  cache_in, cache_in],
        out_specs=[pl.BlockSpec((rows, ATTN_WIDTH), lambda i: (i, 0)), cache, cache],
        out_shape=[jax.ShapeDtypeStruct((N_SAMPLE, ATTN_WIDTH), F32), cache_shape, cache_shape],
        compiler_params=_params("arbitrary"),
        name="attn_sample",
    )(sinks, proj, proj, proj, k_buf, v_buf)


def _cumsum_rows(x):
    n = x.shape[0]
    row = lax.broadcasted_iota(jnp.int32, x.shape, 0)
    step = 1
    while step < n:
        x = x + jnp.where(row >= step, pltpu.roll(x, step, axis=0), 0.0)
        step *= 2
    return x


def _gdn_kernel(*refs, chunk, rows_in, has_state):
    if has_state:
        b1_ref, b2_ref, b3_ref, cw_ref, gp_ref, nw_ref, st_ref, s0_ref, o_ref, s_ref, ext_ref = refs
        first = True
    else:
        b1_ref, b2_ref, b3_ref, cw_ref, gp_ref, nw_ref, o_ref, s_ref, ext_ref = refs
        first = pl.program_id(1) == 0

    if has_state:
        ext_ref[0:8, :] = st_ref[...]
        s_ref[...] = s0_ref[...]
    else:
        @pl.when(first)
        def _():
            ext_ref[0:8, :] = jnp.zeros((8, CONV_CH), F32)
            s_ref[...] = jnp.zeros_like(s_ref)

        @pl.when(jnp.logical_not(first))
        def _():
            ext_ref[0:8, :] = ext_ref[chunk:chunk + 8, :]

    ext_ref[8:8 + rows_in, 0:GDN_COLS] = b1_ref[...]
    ext_ref[8:8 + rows_in, GDN_COLS:CONV_CH] = b2_ref[...]
    if rows_in < chunk:
        ext_ref[8 + rows_in:8 + chunk, :] = jnp.zeros((chunk - rows_in, CONV_CH), F32)

    lane = lax.broadcasted_iota(jnp.int32, (chunk, LANES), 1)
    row = lax.broadcasted_iota(jnp.int32, (chunk, LANES), 0)
    if rows_in < chunk:
        ab = jnp.concatenate([b3_ref[:, DN_VW:DN_VW + LANES],
                              jnp.zeros((chunk - rows_in, LANES), F32)], axis=0)
    else:
        ab = b3_ref[:, DN_VW:DN_VW + LANES]
    live = (lane < 2 * DN_HEADS) & (row < rows_in)
    ab = jnp.where(live, ab, 0.0)
    a_scale = gp_ref[0:1, :]
    dt_bias = gp_ref[1:2, :]
    g_all = jnp.where(live, a_scale * jax.nn.softplus(ab + dt_bias), 0.0)
    beta_all = jnp.where(live, jax.nn.sigmoid(ab), 0.0)
    gcum = _cumsum_rows(g_all)
    gcum_t = gcum.T
    g_last_all = gcum[chunk - 1:chunk, :]

    ri = lax.broadcasted_iota(jnp.int32, (chunk, chunk), 0)
    ci = lax.broadcasted_iota(jnp.int32, (chunk, chunk), 1)
    incl = ri >= ci
    strict = ri > ci

    def conv_strip(col0):
        cols = slice(col0, col0 + LANES)
        acc = ext_ref[8:8 + chunk, cols] * cw_ref[CONV_W - 1:CONV_W, cols]
        for tap in range(1, CONV_W):
            acc = acc + ext_ref[8 - tap:8 - tap + chunk, cols] * cw_ref[CONV_W - 1 - tap:CONV_W - tap, cols]
        return _silu(acc)

    def l2norm(x):
        return x * lax.rsqrt(jnp.sum(x * x, axis=-1, keepdims=True) + RMS_EPS)

    def run_heads(heads):
        q = {h: l2norm(conv_strip(h * DN_DK)) * (DN_DK ** -0.5) for h in heads}
        k = {h: l2norm(conv_strip(DN_KW + h * DN_DK)) for h in heads}
        v = {h: conv_strip(2 * DN_KW + h * DN_DV) for h in heads}
        gc_col = {h: gcum[:, h:h + 1] for h in heads}
        g_last = {h: g_last_all[:, h:h + 1] for h in heads}
        beta = {h: beta_all[:, DN_HEADS + h:DN_HEADS + h + 1] for h in heads}
        decay = {h: jnp.where(incl, jnp.exp(jnp.where(incl, gc_col[h] - gcum_t[h:h + 1, :], 0.0)), 0.0)
                 for h in heads}
        eg = {h: jnp.exp(gc_col[h]) for h in heads}
        kb = {h: k[h] * beta[h] for h in heads}
        kk = {h: _mm_nt(jnp.concatenate([kb[h], q[h]], axis=0), k[h]) for h in heads}
        qk = {h: jnp.where(incl, kk[h][chunk:] * decay[h], 0.0) for h in heads}
        neg_a = {h: jnp.where(strict, -kk[h][:chunk] * decay[h], 0.0) for h in heads}
        base = min(GDN_INV_BASE, chunk)
        p = {h: jnp.where(ri // base == ci // base, neg_a[h], 0.0) for h in heads}
        eye = (ri == ci).astype(F32)
        inv = {h: eye + p[h] for h in heads}
        span = 2
        while span < base:
            p = {h: _mm(p[h], p[h]) for h in heads}
            inv = {h: inv[h] + _mm(inv[h], p[h]) for h in heads}
            span *= 2
        size = base
        while size < chunk:
            lower_left = (ri // (2 * size) == ci // (2 * size)) & (ri // size != ci // size)
            off = {h: jnp.where(lower_left, neg_a[h], 0.0) for h in heads}
            inv = {h: inv[h] + _mm(_mm(inv[h], off[h]), inv[h]) for h in heads}
            size *= 2
        sol = {h: _mm(inv[h], jnp.concatenate([v[h] * beta[h], kb[h] * eg[h]], axis=1)) for h in heads}
        s_prev = {h: s_ref[h] for h in heads}
        ws = {h: _mm(jnp.concatenate([sol[h][:, DN_DV:], q[h] * eg[h]], axis=0), s_prev[h]) for h in heads}
        v_new = {h: sol[h][:, :DN_DV] - ws[h][:chunk] for h in heads}
        o = {h: ws[h][chunk:] + _mm(qk[h], v_new[h]) for h in heads}
        k_dec = {h: k[h] * jnp.exp(g_last[h] - gc_col[h]) for h in heads}
        for h in heads:
            s_ref[h] = s_prev[h] * jnp.exp(g_last[h]) + _mm_tn(k_dec[h], v_new[h])
        for h in heads:
            on = o[h] * lax.rsqrt(jnp.mean(o[h] * o[h], axis=-1, keepdims=True) + RMS_EPS)
            z = b3_ref[:, h * DN_DV:(h + 1) * DN_DV]
            out = on[0:rows_in] * nw_ref[...] * _silu(z)
            o_ref[:, h * DN_DV:(h + 1) * DN_DV] = out.astype(o_ref.dtype)

    for first_head in range(0, DN_HEADS, GDN_HEAD_GROUP):
        run_heads(range(first_head, first_head + GDN_HEAD_GROUP))


def _gdn_gate_params(a_log, dt_bias):
    rows = jnp.stack([-jnp.exp(a_log.astype(F32)), dt_bias.astype(F32)], axis=1)
    return jnp.pad(rows, ((0, 0), (0, 0), (0, LANES - DN_HEADS)))


def _gdn_prompt(proj, conv_w, gp, norm_w, layer):
    nc = SEQ // GDN_CHUNK
    c0 = COL_QKV // GDN_COLS

    def win(c):
        return pl.BlockSpec((GDN_CHUNK, GDN_COLS), lambda b, i: (b * nc + i, c))

    return pl.pallas_call(
        functools.partial(_gdn_kernel, chunk=GDN_CHUNK, rows_in=GDN_CHUNK, has_state=False),
        grid=(BATCH, nc),
        in_specs=[win(c0), win(c0 + 1), win(c0 + 2), _layer_slab((CONV_W, CONV_CH), layer),
                  _layer_slab((2, LANES), layer), _layer_slab((1, DN_DV), layer)],
        out_specs=[pl.BlockSpec((GDN_CHUNK, DN_VW), lambda b, i: (b * nc + i, 0)),
                   pl.BlockSpec((None, DN_HEADS, DN_DK, DN_DV), lambda b, i: (b, 0, 0, 0))],
        out_shape=[jax.ShapeDtypeStruct((N_PROMPT, DN_VW), BF16),
                   jax.ShapeDtypeStruct((BATCH, DN_HEADS, DN_DK, DN_DV), F32)],
        scratch_shapes=[pltpu.VMEM((8 + GDN_CHUNK, CONV_CH), F32)],
        compiler_params=_params("arbitrary", "arbitrary"),
        name="gdn_prompt",
    )(proj, proj, proj, conv_w, gp, norm_w)


def _gdn_sample(proj, conv_w, gp, norm_w, conv_state8, s0, chunk, layer):
    row0 = N_PROMPT // DEC_SEQ
    c0 = COL_QKV // GDN_COLS

    def win(c):
        return pl.BlockSpec((DEC_SEQ, GDN_COLS), lambda b: (row0 + b, c))

    state = pl.BlockSpec((None, DN_HEADS, DN_DK, DN_DV), lambda b: (b, 0, 0, 0))
    return pl.pallas_call(
        functools.partial(_gdn_kernel, chunk=chunk, rows_in=DEC_SEQ, has_state=True),
        grid=(DEC_BATCH,),
        in_specs=[win(c0), win(c0 + 1), win(c0 + 2), _layer_slab((CONV_W, CONV_CH), layer),
                  _layer_slab((2, LANES), layer), _layer_slab((1, DN_DV), layer),
                  pl.BlockSpec((None, None, 8, CONV_CH), lambda b: (layer, b, 0, 0)),
                  pl.BlockSpec((None, None, DN_HEADS, DN_DK, DN_DV), lambda b: (layer, b, 0, 0, 0))],
        out_specs=[pl.BlockSpec((DEC_SEQ, DN_VW), lambda b: (b, 0)), state],
        out_shape=[jax.ShapeDtypeStruct((N_SAMPLE, DN_VW), F32),
                   jax.ShapeDtypeStruct((DEC_BATCH, DN_HEADS, DN_DK, DN_DV), F32)],
        scratch_shapes=[pltpu.VMEM((8 + chunk, CONV_CH), F32)],
        compiler_params=_params("arbitrary"),
        name="gdn_sample",
    )(proj, proj, proj, conv_w, gp, norm_w, conv_state8, s0)


def _out_router_kernel(map_ref, mas_ref, mdp_ref, mds_ref, x_ref, wo_ref, g_ref, b_ref, wr_ref, br_ref,
                       x1_ref, eid_ref, gate_ref, cnt_ref, ma_tail, md_tail):
    n_sub = OUT_TM // OUT_SUB
    tail = slice(OUT_TM - OUT_SUB, OUT_TM)
    is_last = pl.program_id(0) == pl.num_programs(0) - 1

    @pl.when(is_last)
    def _():
        ma_tail[...] = mas_ref[...].astype(BF16)
        md_tail[...] = mds_ref[...].astype(BF16)

    @pl.when(jnp.logical_not(is_last))
    def _():
        ma_tail[...] = map_ref[tail, :]
        md_tail[...] = mdp_ref[tail, :]

    def mix_of(c):
        rows = slice(c * OUT_SUB, (c + 1) * OUT_SUB)
        ma, md = (ma_tail[...], md_tail[...]) if c == n_sub - 1 else (map_ref[rows, :], mdp_ref[rows, :])
        return (jnp.dot(ma, wo_ref[0:ATTN_WIDTH, :], preferred_element_type=F32)
                + jnp.dot(md, wo_ref[ATTN_WIDTH:MIX_WIDTH, :], preferred_element_type=F32))

    mix = mix_of(0)
    counts = jnp.zeros((1, LANES), F32)
    for c in range(n_sub):
        nxt = mix_of(c + 1) if c + 1 < n_sub else None
        rows = slice(c * OUT_SUB, (c + 1) * OUT_SUB)
        x1 = _layer_norm(DEEPNORM_ALPHA * x_ref[rows, :] + mix, g_ref[...], b_ref[...])
        x1_ref[rows, :] = x1
        logits = jnp.dot(x1.astype(BF16), wr_ref[...], preferred_element_type=F32) + br_ref[...]
        eid, gate = _route(logits)
        eid_ref[rows, :] = eid
        gate_ref[rows, :] = gate
        counts = counts + _expert_counts(eid)
        mix = nxt
    row = lax.broadcasted_iota(jnp.int32, (SUBLANES, LANES), 0)
    cnt_ref[...] = jnp.where(row == 0, counts, 0.0)


def _route(logits):
    lane = lax.broadcasted_iota(jnp.int32, logits.shape, 1)
    lane_f = lane.astype(F32)
    far = float(LANES)
    is_grp = lane < N_GROUPS
    grp_max = jnp.max(jnp.where(is_grp, logits, NEG_INF), axis=-1, keepdims=True)
    grp = jnp.min(jnp.where(is_grp & (logits == grp_max), lane_f, far), axis=-1, keepdims=True)
    grp_gate = 1.0 / jnp.sum(jnp.where(is_grp, jnp.exp(logits - grp_max), 0.0), axis=-1, keepdims=True)
    lo = N_GROUPS + EXPERTS_PER_GROUP * grp
    in_grp = (lane_f >= lo) & (lane_f < lo + EXPERTS_PER_GROUP)
    v1 = jnp.max(jnp.where(in_grp, logits, NEG_INF), axis=-1, keepdims=True)
    i1 = jnp.min(jnp.where(in_grp & (logits == v1), lane_f, far), axis=-1, keepdims=True)
    rest = in_grp & (lane_f != i1)
    v2 = jnp.max(jnp.where(rest, logits, NEG_INF), axis=-1, keepdims=True)
    i2 = jnp.min(jnp.where(rest & (logits == v2), lane_f, far), axis=-1, keepdims=True)
    t = jnp.exp(v2 - v1)
    g1 = grp_gate / (1.0 + t)
    g2 = g1 * t
    e1 = (i1 - N_GROUPS).astype(jnp.int32)
    e2 = (i2 - N_GROUPS).astype(jnp.int32)
    return (jnp.where(lane == 0, e1, jnp.where(lane == 1, e2, 0)),
            jnp.where(lane == 0, g1, jnp.where(lane == 1, g2, 0.0)))


def _out_router(mix_a_p, mix_a_s, mix_d_p, mix_d_s, x, wo_b, g, b, wr_b, br, layer):
    assert N_SAMPLE == OUT_SUB and N_PROMPT % OUT_TM == OUT_TM - OUT_SUB and ATTN_WIDTH == DN_VW

    def row(w):
        return pl.BlockSpec((OUT_TM, w), lambda i: (i, 0))

    prompt = pl.BlockSpec((OUT_TM, ATTN_WIDTH), lambda i: (i, 0))
    sample = pl.BlockSpec((N_SAMPLE, ATTN_WIDTH), lambda i: (0, 0))
    return pl.pallas_call(
        _out_router_kernel,
        grid=(N_TOK // OUT_TM,),
        in_specs=[prompt, sample, prompt, sample, row(D_MODEL), _layer_slab((MIX_WIDTH, D_MODEL), layer),
                  _layer_slab((1, D_MODEL), layer), _layer_slab((1, D_MODEL), layer),
                  _layer_slab((D_MODEL, LANES), layer), _layer_slab((1, LANES), layer)],
        out_specs=[row(D_MODEL), row(LANES), row(LANES), pl.BlockSpec((SUBLANES, LANES), lambda i: (i, 0))],
        out_shape=[jax.ShapeDtypeStruct((N_TOK, D_MODEL), F32),
                   jax.ShapeDtypeStruct((N_TOK, LANES), jnp.int32),
                   jax.ShapeDtypeStruct((N_TOK, LANES), F32),
                   jax.ShapeDtypeStruct((N_TOK // OUT_TM * SUBLANES, LANES), F32)],
        scratch_shapes=[pltpu.VMEM((OUT_SUB, ATTN_WIDTH), BF16), pltpu.VMEM((OUT_SUB, DN_VW), BF16)],
        compiler_params=_params("arbitrary", vmem_limit=VMEM_LIMIT_BIG),
        name="out_router",
    )(mix_a_p, mix_a_s, mix_d_p, mix_d_s, x, wo_b, g, b, wr_b, br)


def _expert_counts(eid):
    lane = lax.broadcasted_iota(jnp.int32, eid.shape, 1)
    hot = (lane == eid[:, 0:1]).astype(F32) + (lane == eid[:, 1:2]).astype(F32)
    return jnp.sum(hot, axis=0, keepdims=True)


def _plan_kernel(eid_ref, part_ref, pos_ref, cnt_ref, run_ref, off_ref):
    i = pl.program_id(0)
    lane = lax.broadcasted_iota(jnp.int32, (ROW_TILE, LANES), 1)
    eid = eid_ref[...]
    hot0 = (lane == eid[:, 0:1]).astype(F32)
    hot1 = (lane == eid[:, 1:2]).astype(F32)
    hot = hot0 + hot1

    @pl.when(i == 0)
    def _():
        cnt = jnp.sum(part_ref[...], axis=0, keepdims=True)
        cnt_ref[...] = cnt
        hi = jnp.floor(cnt * (1.0 / 256.0))
        lo = cnt - 256.0 * hi
        r = lax.broadcasted_iota(jnp.int32, (LANES, LANES), 0)
        c = lax.broadcasted_iota(jnp.int32, (LANES, LANES), 1)
        upper = (r < c).astype(F32)
        hi8 = jnp.broadcast_to(hi, (8, LANES))
        lo8 = jnp.broadcast_to(lo, (8, LANES))
        off = 256.0 * _mm(hi8, upper) + _mm(lo8, upper)
        off_ref[...] = off[0:1]
        run_ref[...] = jnp.zeros_like(run_ref)

    r = lax.broadcasted_iota(jnp.int32, (ROW_TILE, ROW_TILE), 0)
    c = lax.broadcasted_iota(jnp.int32, (ROW_TILE, ROW_TILE), 1)
    before = (r > c).astype(F32)
    slot = _mm(before, hot) + run_ref[...] + off_ref[...]
    p0 = jnp.sum(hot0 * slot, axis=-1, keepdims=True).astype(jnp.int32)
    p1 = jnp.sum(hot1 * slot, axis=-1, keepdims=True).astype(jnp.int32)
    pos_ref[...] = jnp.where(lane == 0, p0, jnp.where(lane == 1, p1, 0))
    run_ref[...] += jnp.sum(hot, axis=0, keepdims=True)


def _plan(eid, part_counts):
    n_part = part_counts.shape[0]
    return pl.pallas_call(
        _plan_kernel,
        grid=(N_TOK // ROW_TILE,),
        in_specs=[pl.BlockSpec((ROW_TILE, LANES), lambda i: (i, 0)),
                  pl.BlockSpec((n_part, LANES), lambda i: (0, 0))],
        out_specs=[pl.BlockSpec((ROW_TILE, LANES), lambda i: (i, 0)),
                   pl.BlockSpec((1, LANES), lambda i: (0, 0))],
        out_shape=[jax.ShapeDtypeStruct((N_TOK, LANES), jnp.int32),
                   jax.ShapeDtypeStruct((1, LANES), F32)],
        scratch_shapes=[pltpu.VMEM((1, LANES), F32), pltpu.VMEM((1, LANES), F32)],
        compiler_params=_params("arbitrary"),
        name="moe_plan",
    )(eid, part_counts)


def _work_items(counts):
    counts = counts.astype(jnp.int32)
    ends = jnp.cumsum(counts)
    starts = ends - counts
    first_tile = starts // MOE_TM
    n_tiles = jnp.where(counts > 0, (ends - 1) // MOE_TM - first_tile + 1, 0)
    item_end = jnp.cumsum(n_tiles)
    item_start = item_end - n_tiles
    n_items = item_end[-1]
    w = jnp.minimum(jnp.arange(N_MOE_ITEMS, dtype=jnp.int32), n_items - 1)
    expert = jnp.sum((item_end[None, :] <= w[:, None]).astype(jnp.int32), axis=1)
    tile = first_tile[expert] + (w - item_start[expert])
    prev_tile = jnp.concatenate([jnp.full((1,), -1, jnp.int32), tile[:-1]])
    prev_expert = jnp.concatenate([jnp.full((1,), -1, jnp.int32), expert[:-1]])
    valid = (jnp.arange(N_MOE_ITEMS, dtype=jnp.int32) < n_items).astype(jnp.int32)
    return (tile, expert, (tile != prev_tile).astype(jnp.int32), (expert != prev_expert).astype(jnp.int32),
            valid, starts[expert], ends[expert])


def _slot_tokens_kernel(pos_ref, tok_ref):
    group = 16

    def place(g, carry):
        a0 = g * group
        slots = [pos_ref[a0 + j] for j in range(group)]
        for j in range(group):
            tok_ref[slots[j]] = g * (group // 2) + j // 2
        return carry

    lax.fori_loop(0, N_ASSIGN // group, place, 0)


def _slot_tokens(pos_flat):
    return pl.pallas_call(
        _slot_tokens_kernel,
        grid_spec=pltpu.PrefetchScalarGridSpec(
            num_scalar_prefetch=1,
            grid=(1,),
            in_specs=[],
            out_specs=pl.BlockSpec(memory_space=pltpu.SMEM)),
        out_shape=jax.ShapeDtypeStruct((N_ASSIGN,), jnp.int32),
        compiler_params=_params("arbitrary"),
        name="moe_slot_tokens",
    )(pos_flat)


def _experts_kernel(tile_ref, exp_ref, first_ref, newexp_ref, valid_ref, gstart_ref, gend_ref, tok_ref,
                    x1_ref, wgu_ref, wdn_ref, o_ref, wgu_b, wdn_b, xs_ref, sem):
    w = pl.program_id(0)

    def start_gather(tile, into):
        base = tile * MOE_TM

        def issue(g, carry):
            for j in range(SUBLANES):
                tok = tok_ref[base + g * SUBLANES + j]
                pltpu.make_async_copy(x1_ref.at[tok >> 3, pl.ds(tok & (SUBLANES - 1), 1), :],
                                      xs_ref.at[into, g, pl.ds(j, 1), :], sem.at[into]).start(priority=j % 2)
            return carry

        lax.fori_loop(0, MOE_TM // SUBLANES, issue, 0)

    @pl.when(w == 0)
    def _():
        start_gather(0, 0)

    @pl.when(valid_ref[w] == 1)
    def _():
        tile = tile_ref[w]
        buf = tile % 2

        @pl.when(first_ref[w] == 1)
        def _():
            pltpu.make_async_copy(x1_ref.at[pl.ds(0, MOE_TM // SUBLANES)], xs_ref.at[buf], sem.at[buf]).wait()

            @pl.when(tile + 1 < N_MOE_TILES)
            def _():
                start_gather(tile + 1, 1 - buf)

        @pl.when(newexp_ref[w] == 1)
        def _():
            wgu_b[...] = wgu_ref[...].astype(BF16)
            wdn_b[...] = wdn_ref[...].astype(BF16)

        xs = xs_ref[buf].reshape(MOE_TM, D_MODEL)
        h = jnp.dot(xs.astype(BF16), wgu_b[...], preferred_element_type=F32)
        act = _silu(h[:, :D_FF_EXPERT]) * h[:, D_FF_EXPERT:]
        y = jnp.dot(act.astype(BF16), wdn_b[...], preferred_element_type=F32)
        rows = tile_ref[w] * MOE_TM + lax.broadcasted_iota(jnp.int32, (MOE_TM, 1), 0)
        mine = (rows >= gstart_ref[w]) & (rows < gend_ref[w])

        @pl.when(first_ref[w] == 1)
        def _():
            o_ref[...] = jnp.where(mine, y, 0.0)

        @pl.when(first_ref[w] == 0)
        def _():
            o_ref[...] = jnp.where(mine, y, o_ref[...])


def _experts(items, slot_tokens, x1, w_gu, w_dn, layer):
    return pl.pallas_call(
        _experts_kernel,
        grid_spec=pltpu.PrefetchScalarGridSpec(
            num_scalar_prefetch=8,
            grid=(N_MOE_ITEMS,),
            in_specs=[pl.BlockSpec(memory_space=pl.ANY),
                      pl.BlockSpec((None, None, D_MODEL, 2 * D_FF_EXPERT),
                                   lambda w, t, e, *_: (layer, e[w], 0, 0)),
                      pl.BlockSpec((None, None, D_FF_EXPERT, D_MODEL),
                                   lambda w, t, e, *_: (layer, e[w], 0, 0))],
            out_specs=pl.BlockSpec((MOE_TM, D_MODEL), lambda w, t, e, *_: (t[w], 0)),
            scratch_shapes=[pltpu.VMEM((D_MODEL, 2 * D_FF_EXPERT), BF16),
                            pltpu.VMEM((D_FF_EXPERT, D_MODEL), BF16),
                            pltpu.VMEM((2, MOE_TM // SUBLANES, SUBLANES, D_MODEL), F32),
                            pltpu.SemaphoreType.DMA((2,))]),
        out_shape=jax.ShapeDtypeStruct((N_ASSIGN, D_MODEL), F32),
        compiler_params=_params("arbitrary"),
        name="moe_experts",
    )(*items, slot_tokens, x1, w_gu, w_dn)


def _combine_kernel(pos_ref, ys_ref, x1_ref, gate_ref, g_ref, b_ref, out_a_ref, out_b_ref, rows_ref, sem, *,
                    last_layer):
    i = pl.program_id(0)
    buf = i % 2

    def start_gather(tile, into):
        base = tile * (2 * ROW_TILE)

        def issue(g, carry):
            for j in range(SUBLANES):
                for pick in range(2):
                    slot = pos_ref[base + 2 * (g * SUBLANES + j) + pick]
                    pltpu.make_async_copy(ys_ref.at[slot >> 3, pl.ds(slot & (SUBLANES - 1), 1), :],
                                          rows_ref.at[into, pick, g, pl.ds(j, 1), :],
                                          sem.at[into]).start(priority=pick)
            return carry

        lax.fori_loop(0, ROW_TILE // SUBLANES, issue, 0)

    @pl.when(i == 0)
    def _():
        start_gather(0, 0)

    @pl.when(i + 1 < pl.num_programs(0))
    def _():
        start_gather(i + 1, 1 - buf)

    for pick in range(2):
        pltpu.make_async_copy(ys_ref.at[pl.ds(0, ROW_TILE // SUBLANES)], rows_ref.at[buf, pick], sem.at[buf]).wait()
    gate = gate_ref[...]
    y = (gate[:, 0:1] * rows_ref[buf, 0].reshape(ROW_TILE, D_MODEL)
         + gate[:, 1:2] * rows_ref[buf, 1].reshape(ROW_TILE, D_MODEL))
    x2 = _layer_norm(DEEPNORM_ALPHA * x1_ref[...] + y, g_ref[...], b_ref[...])
    if last_layer:
        is_prompt = i < N_PROMPT // ROW_TILE

        @pl.when(is_prompt)
        def _():
            out_a_ref[...] = x2

        @pl.when(jnp.logical_not(is_prompt))
        def _():
            out_b_ref[...] = x2
    else:
        out_a_ref[...] = x2
        out_b_ref[...] = x2.astype(BF16)


def _combine(pos_flat, ys, x1, gates, g, b, layer):
    row = lambda w: pl.BlockSpec((ROW_TILE, w), lambda i, pos: (i, 0))
    vec = _layer_slab((1, D_MODEL), layer)
    last_layer = layer == DEPTH - 1
    if last_layer:
        assert N_SAMPLE == ROW_TILE
        last_prompt = N_PROMPT // ROW_TILE - 1
        out_specs = [pl.BlockSpec((ROW_TILE, D_MODEL), lambda i, pos: (jnp.minimum(i, last_prompt), 0)),
                     pl.BlockSpec((ROW_TILE, D_MODEL), lambda i, pos: (0, 0))]
        out_shape = [jax.ShapeDtypeStruct((N_PROMPT, D_MODEL), F32), jax.ShapeDtypeStruct((N_SAMPLE, D_MODEL), F32)]
    else:
        out_specs = [row(D_MODEL), row(D_MODEL)]
        out_shape = [jax.ShapeDtypeStruct((N_TOK, D_MODEL), F32), jax.ShapeDtypeStruct((N_TOK, D_MODEL), BF16)]
    return pl.pallas_call(
        functools.partial(_combine_kernel, last_layer=last_layer),
        grid_spec=pltpu.PrefetchScalarGridSpec(
            num_scalar_prefetch=1,
            grid=(N_TOK // ROW_TILE,),
            in_specs=[pl.BlockSpec(memory_space=pl.ANY), row(D_MODEL), row(LANES), vec, vec],
            out_specs=out_specs,
            scratch_shapes=[pltpu.VMEM((2, 2, ROW_TILE // SUBLANES, SUBLANES, D_MODEL), F32),
                            pltpu.SemaphoreType.DMA((2,))]),
        out_shape=out_shape,
        compiler_params=_params("arbitrary"),
        name="moe_combine",
    )(pos_flat, ys, x1, gates, g, b)


def _router_weights(wg, bg, we, be):
    pad = LANES - N_GROUPS - N_EXPERTS
    wr = jnp.concatenate([wg, we, jnp.zeros((DEPTH, D_MODEL, pad), wg.dtype)], axis=-1)
    br = jnp.concatenate([bg, be, jnp.zeros((DEPTH, pad), bg.dtype)], axis=-1)
    return wr.astype(BF16), br.astype(F32).reshape(DEPTH, 1, LANES)


def kernel(x_prompt, x_sample, cache_swa_k, cache_swa_v, state_conv, state_delta, ln_in_g, ln_in_b, w_in, conv_w,
           a_log, dt_bias, dn_norm_w, attn_sinks, w_out, ln1_g, ln1_b, router_group_w, router_group_b,
           router_expert_w, router_expert_b, w_gate_up, w_down, ln2_g, ln2_b):
    x, xb = _ln_in(x_prompt.reshape(N_PROMPT, D_MODEL), x_sample.reshape(N_SAMPLE, D_MODEL), ln_in_g, ln_in_b)
    k_buf = cache_swa_k.reshape(DEPTH, DEC_BATCH, WINDOW, KV_WIDTH)
    v_buf = cache_swa_v.reshape(DEPTH, DEC_BATCH, WINDOW, KV_WIDTH)
    conv_state8 = jnp.pad(state_conv, ((0, 0), (0, 0), (8 - (CONV_W - 1), 0), (0, 0)))
    gp = _gdn_gate_params(a_log, dt_bias)
    norm_w = dn_norm_w.reshape(DEPTH, 1, DN_DV).astype(F32)
    wo_b = w_out.astype(BF16)
    w_in_t = jnp.swapaxes(w_in, 1, 2)
    wr_b, br = _router_weights(router_group_w, router_group_b, router_expert_w, router_expert_b)
    ln1_g, ln1_b, ln2_g, ln2_b = (a.reshape(DEPTH, 1, D_MODEL) for a in (ln1_g, ln1_b, ln2_g, ln2_b))
    kp, vp, cp, sp = [], [], [], []
    kq, vq, cq, sq = [], [], [], []
    for l in range(DEPTH):
        proj = _proj(xb, w_in_t, l)
        sinks = attn_sinks[l].astype(F32)
        o_attn_p = _attn_prompt(proj, sinks)
        o_attn_s, k_s, v_s = _attn_sample(proj, sinks, k_buf, v_buf, l)
        o_dn_p, s_p = _gdn_prompt(proj, conv_w, gp, norm_w, l)
        o_dn_s, s_s = _gdn_sample(proj, conv_w, gp, norm_w, conv_state8, state_delta, GDN_CHUNK_SAMPLE, l)
        x1, eid, gates, part_counts = _out_router(o_attn_p, o_attn_s, o_dn_p, o_dn_s, x, wo_b, ln1_g, ln1_b,
                                                  wr_b, br, l)
        pos, counts = _plan(eid, part_counts)
        pos_flat = pos[:, :2].reshape(N_ASSIGN)
        items = _work_items(counts[0, :N_EXPERTS])
        ys = _experts(items, _slot_tokens(pos_flat), x1.reshape(N_TOK // SUBLANES, SUBLANES, D_MODEL),
                      w_gate_up, w_down, l)
        x, xb = _combine(pos_flat, ys.reshape(N_ASSIGN // SUBLANES, SUBLANES, D_MODEL), x1, gates, ln2_g, ln2_b, l)

        def prompt_tail(rows, c0, c1):
            return jnp.stack([proj[(b + 1) * SEQ - rows:(b + 1) * SEQ, c0:c1] for b in range(BATCH)])

        kp.append(prompt_tail(WINDOW, COL_K, COL_V).reshape(BATCH, WINDOW, KV_HEADS, HEAD_DIM))
        vp.append(prompt_tail(WINDOW, COL_V, COL_QKV).reshape(BATCH, WINDOW, KV_HEADS, HEAD_DIM))
        cp.append(prompt_tail(CONV_W - 1, COL_QKV, COL_Z))
        sp.append(s_p)
        kq.append(k_s.reshape(DEC_BATCH, WINDOW, KV_HEADS, HEAD_DIM))
        vq.append(v_s.reshape(DEC_BATCH, WINDOW, KV_HEADS, HEAD_DIM))
        cq.append(proj[N_PROMPT:, COL_QKV:COL_Z].reshape(DEC_BATCH, DEC_SEQ, CONV_CH)[:, DEC_SEQ - (CONV_W - 1):])
        sq.append(s_s)
    return (x.reshape(BATCH, SEQ, D_MODEL), xb.reshape(DEC_BATCH, DEC_SEQ, D_MODEL),
            jnp.stack(kp), jnp.stack(vp), jnp.stack(cp), jnp.stack(sp),
            jnp.stack(kq), jnp.stack(vq), jnp.stack(cq), jnp.stack(sq))
```

```python
import functools

import jax
import jax.numpy as jnp
from jax import lax
from jax.experimental import pallas as pl
from jax.experimental.pallas import tpu as pltpu

D_MODEL = 2048
BATCH = 4
SEQ = 2048
DEPTH = 4
DEC_BATCH = 32
DEC_SEQ = 8
HEAD_DIM = 128
ATTN_HEADS = 8
KV_HEADS = 2
GQA = ATTN_HEADS // KV_HEADS
ATTN_WIDTH = ATTN_HEADS * HEAD_DIM
KV_WIDTH = KV_HEADS * HEAD_DIM
WINDOW = 128
DN_HEADS = 8
DN_DK = 128
DN_DV = 128
DN_KW = DN_HEADS * DN_DK
DN_VW = DN_HEADS * DN_DV
CONV_W = 4
CONV_CH = 2 * DN_KW + DN_VW
MIX_WIDTH = ATTN_WIDTH + DN_VW
IN_COLS = ATTN_WIDTH + 2 * KV_WIDTH + CONV_CH + DN_VW + 2 * DN_HEADS
N_GROUPS = 4
EXPERTS_PER_GROUP = 8
N_EXPERTS = N_GROUPS * EXPERTS_PER_GROUP
D_FF_EXPERT = 256
DEEPNORM_ALPHA = (2 * DEPTH) ** 0.25
LN_EPS = 1e-5
RMS_EPS = 1e-6
NEG_INF = -1e30

N_PROMPT = BATCH * SEQ
N_SAMPLE = DEC_BATCH * DEC_SEQ
N_TOK = N_PROMPT + N_SAMPLE
N_ASSIGN = 2 * N_TOK

COL_K = ATTN_WIDTH
COL_V = COL_K + KV_WIDTH
COL_QKV = COL_V + KV_WIDTH
COL_Z = COL_QKV + CONV_CH
COL_AB = COL_Z + DN_VW

LANES = 128
SUBLANES = 8
ROW_TILE = 256
PROJ_TM = 1408
PROJ_TN = 1024
OUT_TM = 768
OUT_SUB = 256
GDN_CHUNK = 128
GDN_CHUNK_SAMPLE = 8
GDN_HEAD_GROUP = 8
GDN_INV_BASE = 16
GDN_COLS = 1536
SAMPLE_BB = 4
ATTN_QB = 8
MOE_TM = 256
N_MOE_TILES = N_ASSIGN // MOE_TM
N_MOE_ITEMS = N_MOE_TILES + N_EXPERTS - 1
VMEM_LIMIT = 48 * 1024 * 1024
VMEM_LIMIT_BIG = 56 * 1024 * 1024

F32 = jnp.float32
BF16 = jnp.bfloat16


def _params(*sem, vmem_limit=VMEM_LIMIT):
    return pltpu.CompilerParams(dimension_semantics=sem, vmem_limit_bytes=vmem_limit)


def _mm(a, b):
    return jnp.dot(a.astype(BF16), b.astype(BF16), preferred_element_type=F32)


def _mm_nt(a, b):
    return lax.dot_general(a.astype(BF16), b.astype(BF16), (((1,), (1,)), ((), ())),
                           preferred_element_type=F32)


def _mm_tn(a, b):
    return lax.dot_general(a.astype(BF16), b.astype(BF16), (((0,), (0,)), ((), ())),
                           preferred_element_type=F32)


def _layer_slab(shape, layer):
    return pl.BlockSpec((None,) + tuple(shape), lambda *_: (layer,) + (0,) * len(shape),
                        pipeline_mode=pl.Buffered(1))


def _layer_norm(x, g, b):
    mu = jnp.mean(x, axis=-1, keepdims=True)
    xc = x - mu
    var = jnp.mean(xc * xc, axis=-1, keepdims=True)
    return xc * lax.rsqrt(var + LN_EPS) * g + b


def _silu(x):
    return x / (1.0 + jnp.exp(-x))


def _ln_in_kernel(xp_ref, xs_ref, g_ref, b_ref, o_ref, ob_ref):
    def emit(x_ref):
        y = _layer_norm(x_ref[...], g_ref[...], b_ref[...])
        o_ref[...] = y
        ob_ref[...] = y.astype(BF16)

    is_prompt = pl.program_id(0) < N_PROMPT // ROW_TILE
    pl.when(is_prompt)(lambda: emit(xp_ref))
    pl.when(jnp.logical_not(is_prompt))(lambda: emit(xs_ref))


def _ln_in(xp, xs, g, b):
    assert N_SAMPLE == ROW_TILE
    row = pl.BlockSpec((ROW_TILE, D_MODEL), lambda i: (i, 0))
    vec = pl.BlockSpec((1, D_MODEL), lambda i: (0, 0))
    last_prompt = N_PROMPT // ROW_TILE - 1
    return pl.pallas_call(
        _ln_in_kernel,
        grid=(N_TOK // ROW_TILE,),
        in_specs=[pl.BlockSpec((ROW_TILE, D_MODEL), lambda i: (jnp.minimum(i, last_prompt), 0)),
                  pl.BlockSpec((ROW_TILE, D_MODEL), lambda i: (0, 0)), vec, vec],
        out_specs=[row, row],
        out_shape=[jax.ShapeDtypeStruct((N_TOK, D_MODEL), F32),
                   jax.ShapeDtypeStruct((N_TOK, D_MODEL), BF16)],
        compiler_params=_params("arbitrary"),
        name="ln_in",
    )(xp, xs, g.reshape(1, D_MODEL), b.reshape(1, D_MODEL))


def _proj_kernel(x_ref, wt_ref, o_ref, wb_ref):
    @pl.when(pl.program_id(1) == 0)
    def _():
        wb_ref[...] = wt_ref[...].astype(BF16)

    o_ref[...] = lax.dot_general(x_ref[...], wb_ref[...], (((1,), (1,)), ((), ())), preferred_element_type=F32)


def _proj(xb, w_in_t, layer):
    return pl.pallas_call(
        _proj_kernel,
        grid=(pl.cdiv(IN_COLS, PROJ_TN), N_TOK // PROJ_TM),
        in_specs=[pl.BlockSpec((PROJ_TM, D_MODEL), lambda j, i: (i, 0)),
                  pl.BlockSpec((None, PROJ_TN, D_MODEL), lambda j, i: (layer, j, 0))],
        out_specs=pl.BlockSpec((PROJ_TM, PROJ_TN), lambda j, i: (i, j)),
        out_shape=jax.ShapeDtypeStruct((N_TOK, IN_COLS), F32),
        scratch_shapes=[pltpu.VMEM((PROJ_TN, D_MODEL), BF16)],
        compiler_params=_params("arbitrary", "arbitrary", vmem_limit=VMEM_LIMIT_BIG),
        name="in_proj",
    )(xb, w_in_t)


def _head_column(values, rows_per_head, n_rows):
    grp = lax.broadcasted_iota(jnp.int32, (n_rows, 1), 0) // rows_per_head
    col = jnp.full((n_rows, 1), values[-1], F32)
    for g in range(len(values) - 2, -1, -1):
        col = jnp.where(grp == g, values[g], col)
    return col


def _softmax_with_sink(s, sink_col):
    m = jnp.maximum(jnp.max(s, axis=-1, keepdims=True), sink_col)
    p = jnp.exp(s - m)
    denom = jnp.sum(p, axis=-1, keepdims=True) + jnp.exp(sink_col - m)
    return p, denom


def _attn_prompt_kernel(sink_ref, q_ref, kc_ref, kp_ref, vc_ref, vp_ref, o_ref):
    step = pl.program_id(1)
    n_rows = GQA * WINDOW
    qpos = lax.broadcasted_iota(jnp.int32, (n_rows, 2 * WINDOW), 0) % WINDOW
    kidx = lax.broadcasted_iota(jnp.int32, (n_rows, 2 * WINDOW), 1)
    dist_i = qpos + WINDOW - kidx
    in_window = (dist_i >= 0) & (dist_i <= WINDOW)
    dist = dist_i.astype(F32)
    for sub in range(ATTN_QB):
        rows = slice(sub * WINDOW, (sub + 1) * WINDOW)
        before = slice((sub - 1) * WINDOW, sub * WINDOW)
        valid = in_window if sub > 0 else in_window & ((kidx >= WINDOW) | (step > 0))
        for kvh in range(KV_HEADS):
            cols = slice(kvh * HEAD_DIM, (kvh + 1) * HEAD_DIM)
            k_before = kc_ref[before, cols] if sub > 0 else kp_ref[:, cols]
            v_before = vc_ref[before, cols] if sub > 0 else vp_ref[:, cols]
            kk = jnp.concatenate([k_before, kc_ref[rows, cols]], axis=0)
            vv = jnp.concatenate([v_before, vc_ref[rows, cols]], axis=0)
            heads = [kvh * GQA + g for g in range(GQA)]
            qs = jnp.concatenate([q_ref[rows, h * HEAD_DIM:(h + 1) * HEAD_DIM] for h in heads], axis=0)
            slope = _head_column([2.0 ** (-8.0 * (h + 1) / ATTN_HEADS) for h in heads], WINDOW, n_rows)
            sink = _head_column([sink_ref[h] for h in heads], WINDOW, n_rows)
            s = _mm_nt(qs, kk) * (HEAD_DIM ** -0.5) - slope * dist
            s = jnp.where(valid, s, NEG_INF)
            p, denom = _softmax_with_sink(s, sink)
            o = _mm(p, vv) / denom
            for g, h in enumerate(heads):
                o_ref[rows, h * HEAD_DIM:(h + 1) * HEAD_DIM] = o[g * WINDOW:(g + 1) * WINDOW].astype(BF16)


def _attn_prompt(proj, sinks):
    nb = SEQ // WINDOW
    ns = nb // ATTN_QB
    rows = ATTN_QB * WINDOW
    kcol, vcol = COL_K // KV_WIDTH, COL_V // KV_WIDTH

    def cur(c):
        return lambda b, i: (b * ns + i, c)

    def prev(c):
        return lambda b, i: (jnp.maximum(b * nb + i * ATTN_QB - 1, 0), c)

    return pl.pallas_call(
        _attn_prompt_kernel,
        grid=(BATCH, ns),
        in_specs=[pl.BlockSpec(memory_space=pltpu.SMEM),
                  pl.BlockSpec((rows, ATTN_WIDTH), cur(0)),
                  pl.BlockSpec((rows, KV_WIDTH), cur(kcol)),
                  pl.BlockSpec((WINDOW, KV_WIDTH), prev(kcol)),
                  pl.BlockSpec((rows, KV_WIDTH), cur(vcol)),
                  pl.BlockSpec((WINDOW, KV_WIDTH), prev(vcol))],
        out_specs=pl.BlockSpec((rows, ATTN_WIDTH), lambda b, i: (b * ns + i, 0)),
        out_shape=jax.ShapeDtypeStruct((N_PROMPT, ATTN_WIDTH), BF16),
        compiler_params=_params("arbitrary", "arbitrary"),
        name="attn_prompt",
    )(sinks, proj, proj, proj, proj, proj)


def _attn_sample_kernel(sink_ref, q_ref, kn_ref, vn_ref, kb_ref, vb_ref, o_ref, ko_ref, vo_ref):
    n_rows = GQA * DEC_SEQ
    tpos = lax.broadcasted_iota(jnp.int32, (n_rows, 2 * WINDOW), 0) % DEC_SEQ
    kidx = lax.broadcasted_iota(jnp.int32, (n_rows, 2 * WINDOW), 1)
    dist_i = jnp.where(kidx < WINDOW, tpos + WINDOW - kidx, tpos - (kidx - WINDOW))
    valid = (dist_i >= 0) & (dist_i <= WINDOW) & (kidx < WINDOW + DEC_SEQ)
    dist = dist_i.astype(F32)
    pad = jnp.zeros((WINDOW - DEC_SEQ, HEAD_DIM), F32)
    for bb in range(SAMPLE_BB):
        rows = slice(bb * DEC_SEQ, (bb + 1) * DEC_SEQ)
        for kvh in range(KV_HEADS):
            cols = slice(kvh * HEAD_DIM, (kvh + 1) * HEAD_DIM)
            kk = jnp.concatenate([kb_ref[bb, :, cols], kn_ref[rows, cols], pad], axis=0)
            vv = jnp.concatenate([vb_ref[bb, :, cols], vn_ref[rows, cols], pad], axis=0)
            heads = [kvh * GQA + g for g in range(GQA)]
            qs = jnp.concatenate([q_ref[rows, h * HEAD_DIM:(h + 1) * HEAD_DIM] for h in heads], axis=0)
            slope = _head_column([2.0 ** (-8.0 * (h + 1) / ATTN_HEADS) for h in heads], DEC_SEQ, n_rows)
            sink = _head_column([sink_ref[h] for h in heads], DEC_SEQ, n_rows)
            s = _mm_nt(qs, kk) * (HEAD_DIM ** -0.5) - slope * dist
            s = jnp.where(valid, s, NEG_INF)
            p, denom = _softmax_with_sink(s, sink)
            o = _mm(p, vv) / denom
            for g, h in enumerate(heads):
                o_ref[rows, h * HEAD_DIM:(h + 1) * HEAD_DIM] = o[g * DEC_SEQ:(g + 1) * DEC_SEQ]
        ko_ref[bb, 0:WINDOW - DEC_SEQ, :] = kb_ref[bb, DEC_SEQ:WINDOW, :]
        ko_ref[bb, WINDOW - DEC_SEQ:WINDOW, :] = kn_ref[rows, :]
        vo_ref[bb, 0:WINDOW - DEC_SEQ, :] = vb_ref[bb, DEC_SEQ:WINDOW, :]
        vo_ref[bb, WINDOW - DEC_SEQ:WINDOW, :] = vn_ref[rows, :]


def _attn_sample(proj, sinks, k_buf, v_buf, layer):
    rows = SAMPLE_BB * DEC_SEQ
    row0 = N_PROMPT // rows
    kcol, vcol = COL_K // KV_WIDTH, COL_V // KV_WIDTH
    cache_in = pl.BlockSpec((None, SAMPLE_BB, WINDOW, KV_WIDTH), lambda i: (layer, i, 0, 0))
    cache = pl.BlockSpec((SAMPLE_BB, WINDOW, KV_WIDTH), lambda i: (i, 0, 0))
    cache_shape = jax.ShapeDtypeStruct((DEC_BATCH, WINDOW, KV_WIDTH), F32)
    return pl.pallas_call(
        _attn_sample_kernel,
        grid=(DEC_BATCH // SAMPLE_BB,),
        in_specs=[pl.BlockSpec(memory_space=pltpu.SMEM),
                  pl.BlockSpec((rows, ATTN_WIDTH), lambda i: (row0 + i, 0)),
                  pl.BlockSpec((rows, KV_WIDTH), lambda i: (row0 + i, kcol)),
                  pl.BlockSpec((rows, KV_WIDTH), lambda i: (row0 + i, vcol)),
                  cache_in, cache_in],
        out_specs=[pl.BlockSpec((rows, ATTN_WIDTH), lambda i: (i, 0)), cache, cache],
        out_shape=[jax.ShapeDtypeStruct((N_SAMPLE, ATTN_WIDTH), F32), cache_shape, cache_shape],
        compiler_params=_params("arbitrary"),
        name="attn_sample",
    )(sinks, proj, proj, proj, k_buf, v_buf)


def _cumsum_rows(x):
    n = x.shape[0]
    row = lax.broadcasted_iota(jnp.int32, x.shape, 0)
    step = 1
    while step < n:
        x = x + jnp.where(row >= step, pltpu.roll(x, step, axis=0), 0.0)
        step *= 2
    return x


def _gdn_kernel(*refs, chunk, rows_in, has_state):
    if has_state:
        b1_ref, b2_ref, b3_ref, cw_ref, gp_ref, nw_ref, st_ref, s0_ref, o_ref, s_ref, ext_ref = refs
        first = True
    else:
        b1_ref, b2_ref, b3_ref, cw_ref, gp_ref, nw_ref, o_ref, s_ref, ext_ref = refs
        first = pl.program_id(1) == 0

    if has_state:
        ext_ref[0:8, :] = st_ref[...]
        s_ref[...] = s0_ref[...]
    else:
        @pl.when(first)
        def _():
            ext_ref[0:8, :] = jnp.zeros((8, CONV_CH), F32)
            s_ref[...] = jnp.zeros_like(s_ref)

        @pl.when(jnp.logical_not(first))
        def _():
            ext_ref[0:8, :] = ext_ref[chunk:chunk + 8, :]

    ext_ref[8:8 + rows_in, 0:GDN_COLS] = b1_ref[...]
    ext_ref[8:8 + rows_in, GDN_COLS:CONV_CH] = b2_ref[...]
    if rows_in < chunk:
        ext_ref[8 + rows_in:8 + chunk, :] = jnp.zeros((chunk - rows_in, CONV_CH), F32)

    lane = lax.broadcasted_iota(jnp.int32, (chunk, LANES), 1)
    row = lax.broadcasted_iota(jnp.int32, (chunk, LANES), 0)
    if rows_in < chunk:
        ab = jnp.concatenate([b3_ref[:, DN_VW:DN_VW + LANES],
                              jnp.zeros((chunk - rows_in, LANES), F32)], axis=0)
    else:
        ab = b3_ref[:, DN_VW:DN_VW + LANES]
    live = (lane < 2 * DN_HEADS) & (row < rows_in)
    ab = jnp.where(live, ab, 0.0)
    a_scale = gp_ref[0:1, :]
    dt_bias = gp_ref[1:2, :]
    g_all = jnp.where(live, a_scale * jax.nn.softplus(ab + dt_bias), 0.0)
    beta_all = jnp.where(live, jax.nn.sigmoid(ab), 0.0)
    gcum = _cumsum_rows(g_all)
    gcum_t = gcum.T
    g_last_all = gcum[chunk - 1:chunk, :]

    ri = lax.broadcasted_iota(jnp.int32, (chunk, chunk), 0)
    ci = lax.broadcasted_iota(jnp.int32, (chunk, chunk), 1)
    incl = ri >= ci
    strict = ri > ci

    def conv_strip(col0):
        cols = slice(col0, col0 + LANES)
        acc = ext_ref[8:8 + chunk, cols] * cw_ref[CONV_W - 1:CONV_W, cols]
        for tap in range(1, CONV_W):
            acc = acc + ext_ref[8 - tap:8 - tap + chunk, cols] * cw_ref[CONV_W - 1 - tap:CONV_W - tap, cols]
        return _silu(acc)

    def l2norm(x):
        return x * lax.rsqrt(jnp.sum(x * x, axis=-1, keepdims=True) + RMS_EPS)

    def run_heads(heads):
        q = {h: l2norm(conv_strip(h * DN_DK)) * (DN_DK ** -0.5) for h in heads}
        k = {h: l2norm(conv_strip(DN_KW + h * DN_DK)) for h in heads}
        v = {h: conv_strip(2 * DN_KW + h * DN_DV) for h in heads}
        gc_col = {h: gcum[:, h:h + 1] for h in heads}
        g_last = {h: g_last_all[:, h:h + 1] for h in heads}
        beta = {h: beta_all[:, DN_HEADS + h:DN_HEADS + h + 1] for h in heads}
        decay = {h: jnp.where(incl, jnp.exp(jnp.where(incl, gc_col[h] - gcum_t[h:h + 1, :], 0.0)), 0.0)
                 for h in heads}
        eg = {h: jnp.exp(gc_col[h]) for h in heads}
        kb = {h: k[h] * beta[h] for h in heads}
        kk = {h: _mm_nt(jnp.concatenate([kb[h], q[h]], axis=0), k[h]) for h in heads}
        qk = {h: jnp.where(incl, kk[h][chunk:] * decay[h], 0.0) for h in heads}
        neg_a = {h: jnp.where(strict, -kk[h][:chunk] * decay[h], 0.0) for h in heads}
        base = min(GDN_INV_BASE, chunk)
        p = {h: jnp.where(ri // base == ci // base, neg_a[h], 0.0) for h in heads}
        eye = (ri == ci).astype(F32)
        inv = {h: eye + p[h] for h in heads}
        span = 2
        while span < base:
            p = {h: _mm(p[h], p[h]) for h in heads}
            inv = {h: inv[h] + _mm(inv[h], p[h]) for h in heads}
            span *= 2
        size = base
        while size < chunk:
            lower_left = (ri // (2 * size) == ci // (2 * size)) & (ri // size != ci // size)
            off = {h: jnp.where(lower_left, neg_a[h], 0.0) for h in heads}
            inv = {h: inv[h] + _mm(_mm(inv[h], off[h]), inv[h]) for h in heads}
            size *= 2
        sol = {h: _mm(inv[h], jnp.concatenate([v[h] * beta[h], kb[h] * eg[h]], axis=1)) for h in heads}
        s_prev = {h: s_ref[h] for h in heads}
        ws = {h: _mm(jnp.concatenate([sol[h][:, DN_DV:], q[h] * eg[h]], axis=0), s_prev[h]) for h in heads}
        v_new = {h: sol[h][:, :DN_DV] - ws[h][:chunk] for h in heads}
        o = {h: ws[h][chunk:] + _mm(qk[h], v_new[h]) for h in heads}
        k_dec = {h: k[h] * jnp.exp(g_last[h] - gc_col[h]) for h in heads}
        for h in heads:
            s_ref[h] = s_prev[h] * jnp.exp(g_last[h]) + _mm_tn(k_dec[h], v_new[h])
        for h in heads:
            on = o[h] * lax.rsqrt(jnp.mean(o[h] * o[h], axis=-1, keepdims=True) + RMS_EPS)
            z = b3_ref[:, h * DN_DV:(h + 1) * DN_DV]
            out = on[0:rows_in] * nw_ref[...] * _silu(z)
            o_ref[:, h * DN_DV:(h + 1) * DN_DV] = out.astype(o_ref.dtype)

    for first_head in range(0, DN_HEADS, GDN_HEAD_GROUP):
        run_heads(range(first_head, first_head + GDN_HEAD_GROUP))


def _gdn_gate_params(a_log, dt_bias):
    rows = jnp.stack([-jnp.exp(a_log.astype(F32)), dt_bias.astype(F32)], axis=1)
    return jnp.pad(rows, ((0, 0), (0, 0), (0, LANES - DN_HEADS)))


def _gdn_prompt(proj, conv_w, gp, norm_w, layer):
    nc = SEQ // GDN_CHUNK
    c0 = COL_QKV // GDN_COLS

    def win(c):
        return pl.BlockSpec((GDN_CHUNK, GDN_COLS), lambda b, i: (b * nc + i, c))

    return pl.pallas_call(
        functools.partial(_gdn_kernel, chunk=GDN_CHUNK, rows_in=GDN_CHUNK, has_state=False),
        grid=(BATCH, nc),
        in_specs=[win(c0), win(c0 + 1), win(c0 + 2), _layer_slab((CONV_W, CONV_CH), layer),
                  _layer_slab((2, LANES), layer), _layer_slab((1, DN_DV), layer)],
        out_specs=[pl.BlockSpec((GDN_CHUNK, DN_VW), lambda b, i: (b * nc + i, 0)),
                   pl.BlockSpec((None, DN_HEADS, DN_DK, DN_DV), lambda b, i: (b, 0, 0, 0))],
        out_shape=[jax.ShapeDtypeStruct((N_PROMPT, DN_VW), BF16),
                   jax.ShapeDtypeStruct((BATCH, DN_HEADS, DN_DK, DN_DV), F32)],
        scratch_shapes=[pltpu.VMEM((8 + GDN_CHUNK, CONV_CH), F32)],
        compiler_params=_params("arbitrary", "arbitrary"),
        name="gdn_prompt",
    )(proj, proj, proj, conv_w, gp, norm_w)


def _gdn_sample(proj, conv_w, gp, norm_w, conv_state8, s0, chunk, layer):
    row0 = N_PROMPT // DEC_SEQ
    c0 = COL_QKV // GDN_COLS

    def win(c):
        return pl.BlockSpec((DEC_SEQ, GDN_COLS), lambda b: (row0 + b, c))

    state = pl.BlockSpec((None, DN_HEADS, DN_DK, DN_DV), lambda b: (b, 0, 0, 0))
    return pl.pallas_call(
        functools.partial(_gdn_kernel, chunk=chunk, rows_in=DEC_SEQ, has_state=True),
        grid=(DEC_BATCH,),
        in_specs=[win(c0), win(c0 + 1), win(c0 + 2), _layer_slab((CONV_W, CONV_CH), layer),
                  _layer_slab((2, LANES), layer), _layer_slab((1, DN_DV), layer),
                  pl.BlockSpec((None, None, 8, CONV_CH), lambda b: (layer, b, 0, 0)),
                  pl.BlockSpec((None, None, DN_HEADS, DN_DK, DN_DV), lambda b: (layer, b, 0, 0, 0))],
        out_specs=[pl.BlockSpec((DEC_SEQ, DN_VW), lambda b: (b, 0)), state],
        out_shape=[jax.ShapeDtypeStruct((N_SAMPLE, DN_VW), F32),
                   jax.ShapeDtypeStruct((DEC_BATCH, DN_HEADS, DN_DK, DN_DV), F32)],
        scratch_shapes=[pltpu.VMEM((8 + chunk, CONV_CH), F32)],
        compiler_params=_params("arbitrary"),
        name="gdn_sample",
    )(proj, proj, proj, conv_w, gp, norm_w, conv_state8, s0)


def _out_router_kernel(map_ref, mas_ref, mdp_ref, mds_ref, x_ref, wo_ref, g_ref, b_ref, wr_ref, br_ref,
                       x1_ref, eid_ref, gate_ref, cnt_ref, ma_tail, md_tail):
    n_sub = OUT_TM // OUT_SUB
    tail = slice(OUT_TM - OUT_SUB, OUT_TM)
    is_last = pl.program_id(0) == pl.num_programs(0) - 1

    @pl.when(is_last)
    def _():
        ma_tail[...] = mas_ref[...].astype(BF16)
        md_tail[...] = mds_ref[...].astype(BF16)

    @pl.when(jnp.logical_not(is_last))
    def _():
        ma_tail[...] = map_ref[tail, :]
        md_tail[...] = mdp_ref[tail, :]

    def mix_of(c):
        rows = slice(c * OUT_SUB, (c + 1) * OUT_SUB)
        ma, md = (ma_tail[...], md_tail[...]) if c == n_sub - 1 else (map_ref[rows, :], mdp_ref[rows, :])
        return (jnp.dot(ma, wo_ref[0:ATTN_WIDTH, :], preferred_element_type=F32)
                + jnp.dot(md, wo_ref[ATTN_WIDTH:MIX_WIDTH, :], preferred_element_type=F32))

    mix = mix_of(0)
    counts = jnp.zeros((1, LANES), F32)
    for c in range(n_sub):
        nxt = mix_of(c + 1) if c + 1 < n_sub else None
        rows = slice(c * OUT_SUB, (c + 1) * OUT_SUB)
        x1 = _layer_norm(DEEPNORM_ALPHA * x_ref[rows, :] + mix, g_ref[...], b_ref[...])
        x1_ref[rows, :] = x1
        logits = jnp.dot(x1.astype(BF16), wr_ref[...], preferred_element_type=F32) + br_ref[...]
        eid, gate = _route(logits)
        eid_ref[rows, :] = eid
        gate_ref[rows, :] = gate
        counts = counts + _expert_counts(eid)
        mix = nxt
    row = lax.broadcasted_iota(jnp.int32, (SUBLANES, LANES), 0)
    cnt_ref[...] = jnp.where(row == 0, counts, 0.0)


def _route(logits):
    lane = lax.broadcasted_iota(jnp.int32, logits.shape, 1)
    lane_f = lane.astype(F32)
    far = float(LANES)
    is_grp = lane < N_GROUPS
    grp_max = jnp.max(jnp.where(is_grp, logits, NEG_INF), axis=-1, keepdims=True)
    grp = jnp.min(jnp.where(is_grp & (logits == grp_max), lane_f, far), axis=-1, keepdims=True)
    grp_gate = 1.0 / jnp.sum(jnp.where(is_grp, jnp.exp(logits - grp_max), 0.0), axis=-1, keepdims=True)
    lo = N_GROUPS + EXPERTS_PER_GROUP * grp
    in_grp = (lane_f >= lo) & (lane_f < lo + EXPERTS_PER_GROUP)
    v1 = jnp.max(jnp.where(in_grp, logits, NEG_INF), axis=-1, keepdims=True)
    i1 = jnp.min(jnp.where(in_grp & (logits == v1), lane_f, far), axis=-1, keepdims=True)
    rest = in_grp & (lane_f != i1)
    v2 = jnp.max(jnp.where(rest, logits, NEG_INF), axis=-1, keepdims=True)
    i2 = jnp.min(jnp.where(rest & (logits == v2), lane_f, far), axis=-1, keepdims=True)
    t = jnp.exp(v2 - v1)
    g1 = grp_gate / (1.0 + t)
    g2 = g1 * t
    e1 = (i1 - N_GROUPS).astype(jnp.int32)
    e2 = (i2 - N_GROUPS).astype(jnp.int32)
    return (jnp.where(lane == 0, e1, jnp.where(lane == 1, e2, 0)),
            jnp.where(lane == 0, g1, jnp.where(lane == 1, g2, 0.0)))


def _out_router(mix_a_p, mix_a_s, mix_d_p, mix_d_s, x, wo_b, g, b, wr_b, br, layer):
    assert N_SAMPLE == OUT_SUB and N_PROMPT % OUT_TM == OUT_TM - OUT_SUB and ATTN_WIDTH == DN_VW

    def row(w):
        return pl.BlockSpec((OUT_TM, w), lambda i: (i, 0))

    prompt = pl.BlockSpec((OUT_TM, ATTN_WIDTH), lambda i: (i, 0))
    sample = pl.BlockSpec((N_SAMPLE, ATTN_WIDTH), lambda i: (0, 0))
    return pl.pallas_call(
        _out_router_kernel,
        grid=(N_TOK // OUT_TM,),
        in_specs=[prompt, sample, prompt, sample, row(D_MODEL), _layer_slab((MIX_WIDTH, D_MODEL), layer),
                  _layer_slab((1, D_MODEL), layer), _layer_slab((1, D_MODEL), layer),
                  _layer_slab((D_MODEL, LANES), layer), _layer_slab((1, LANES), layer)],
        out_specs=[row(D_MODEL), row(LANES), row(LANES), pl.BlockSpec((SUBLANES, LANES), lambda i: (i, 0))],
        out_shape=[jax.ShapeDtypeStruct((N_TOK, D_MODEL), F32),
                   jax.ShapeDtypeStruct((N_TOK, LANES), jnp.int32),
                   jax.ShapeDtypeStruct((N_TOK, LANES), F32),
                   jax.ShapeDtypeStruct((N_TOK // OUT_TM * SUBLANES, LANES), F32)],
        scratch_shapes=[pltpu.VMEM((OUT_SUB, ATTN_WIDTH), BF16), pltpu.VMEM((OUT_SUB, DN_VW), BF16)],
        compiler_params=_params("arbitrary", vmem_limit=VMEM_LIMIT_BIG),
        name="out_router",
    )(mix_a_p, mix_a_s, mix_d_p, mix_d_s, x, wo_b, g, b, wr_b, br)


def _expert_counts(eid):
    lane = lax.broadcasted_iota(jnp.int32, eid.shape, 1)
    hot = (lane == eid[:, 0:1]).astype(F32) + (lane == eid[:, 1:2]).astype(F32)
    return jnp.sum(hot, axis=0, keepdims=True)


def _plan_kernel(eid_ref, part_ref, pos_ref, cnt_ref, run_ref, off_ref):
    i = pl.program_id(0)
    lane = lax.broadcasted_iota(jnp.int32, (ROW_TILE, LANES), 1)
    eid = eid_ref[...]
    hot0 = (lane == eid[:, 0:1]).astype(F32)
    hot1 = (lane == eid[:, 1:2]).astype(F32)
    hot = hot0 + hot1

    @pl.when(i == 0)
    def _():
        cnt = jnp.sum(part_ref[...], axis=0, keepdims=True)
        cnt_ref[...] = cnt
        hi = jnp.floor(cnt * (1.0 / 256.0))
        lo = cnt - 256.0 * hi
        r = lax.broadcasted_iota(jnp.int32, (LANES, LANES), 0)
        c = lax.broadcasted_iota(jnp.int32, (LANES, LANES), 1)
        upper = (r < c).astype(F32)
        hi8 = jnp.broadcast_to(hi, (8, LANES))
        lo8 = jnp.broadcast_to(lo, (8, LANES))
        off = 256.0 * _mm(hi8, upper) + _mm(lo8, upper)
        off_ref[...] = off[0:1]
        run_ref[...] = jnp.zeros_like(run_ref)

    r = lax.broadcasted_iota(jnp.int32, (ROW_TILE, ROW_TILE), 0)
    c = lax.broadcasted_iota(jnp.int32, (ROW_TILE, ROW_TILE), 1)
    before = (r > c).astype(F32)
    slot = _mm(before, hot) + run_ref[...] + off_ref[...]
    p0 = jnp.sum(hot0 * slot, axis=-1, keepdims=True).astype(jnp.int32)
    p1 = jnp.sum(hot1 * slot, axis=-1, keepdims=True).astype(jnp.int32)
    pos_ref[...] = jnp.where(lane == 0, p0, jnp.where(lane == 1, p1, 0))
    run_ref[...] += jnp.sum(hot, axis=0, keepdims=True)


def _plan(eid, part_counts):
    n_part = part_counts.shape[0]
    return pl.pallas_call(
        _plan_kernel,
        grid=(N_TOK // ROW_TILE,),
        in_specs=[pl.BlockSpec((ROW_TILE, LANES), lambda i: (i, 0)),
                  pl.BlockSpec((n_part, LANES), lambda i: (0, 0))],
        out_specs=[pl.BlockSpec((ROW_TILE, LANES), lambda i: (i, 0)),
                   pl.BlockSpec((1, LANES), lambda i: (0, 0))],
        out_shape=[jax.ShapeDtypeStruct((N_TOK, LANES), jnp.int32),
                   jax.ShapeDtypeStruct((1, LANES), F32)],
        scratch_shapes=[pltpu.VMEM((1, LANES), F32), pltpu.VMEM((1, LANES), F32)],
        compiler_params=_params("arbitrary"),
        name="moe_plan",
    )(eid, part_counts)


def _work_items(counts):
    counts = counts.astype(jnp.int32)
    ends = jnp.cumsum(counts)
    starts = ends - counts
    first_tile = starts // MOE_TM
    n_tiles = jnp.where(counts > 0, (ends - 1) // MOE_TM - first_tile + 1, 0)
    item_end = jnp.cumsum(n_tiles)
    item_start = item_end - n_tiles
    n_items = item_end[-1]
    w = jnp.minimum(jnp.arange(N_MOE_ITEMS, dtype=jnp.int32), n_items - 1)
    expert = jnp.sum((item_end[None, :] <= w[:, None]).astype(jnp.int32), axis=1)
    tile = first_tile[expert] + (w - item_start[expert])
    prev_tile = jnp.concatenate([jnp.full((1,), -1, jnp.int32), tile[:-1]])
    prev_expert = jnp.concatenate([jnp.full((1,), -1, jnp.int32), expert[:-1]])
    valid = (jnp.arange(N_MOE_ITEMS, dtype=jnp.int32) < n_items).astype(jnp.int32)
    return (tile, expert, (tile != prev_tile).astype(jnp.int32), (expert != prev_expert).astype(jnp.int32),
            valid, starts[expert], ends[expert])


def _slot_tokens_kernel(pos_ref, tok_ref):
    group = 16

    def place(g, carry):
        a0 = g * group
        slots = [pos_ref[a0 + j] for j in range(group)]
        for j in range(group):
            tok_ref[slots[j]] = g * (group // 2) + j // 2
        return carry

    lax.fori_loop(0, N_ASSIGN // group, place, 0)


def _slot_tokens(pos_flat):
    return pl.pallas_call(
        _slot_tokens_kernel,
        grid_spec=pltpu.PrefetchScalarGridSpec(
            num_scalar_prefetch=1,
            grid=(1,),
            in_specs=[],
            out_specs=pl.BlockSpec(memory_space=pltpu.SMEM)),
        out_shape=jax.ShapeDtypeStruct((N_ASSIGN,), jnp.int32),
        compiler_params=_params("arbitrary"),
        name="moe_slot_tokens",
    )(pos_flat)


def _experts_kernel(tile_ref, exp_ref, first_ref, newexp_ref, valid_ref, gstart_ref, gend_ref, tok_ref,
                    x1_ref, wgu_ref, wdn_ref, o_ref, wgu_b, wdn_b, xs_ref, sem):
    w = pl.program_id(0)

    def start_gather(tile, into):
        base = tile * MOE_TM

        def issue(g, carry):
            for j in range(SUBLANES):
                tok = tok_ref[base + g * SUBLANES + j]
                pltpu.make_async_copy(x1_ref.at[tok >> 3, pl.ds(tok & (SUBLANES - 1), 1), :],
                                      xs_ref.at[into, g, pl.ds(j, 1), :], sem.at[into]).start(priority=j % 2)
            return carry

        lax.fori_loop(0, MOE_TM // SUBLANES, issue, 0)

    @pl.when(w == 0)
    def _():
        start_gather(0, 0)

    @pl.when(valid_ref[w] == 1)
    def _():
        tile = tile_ref[w]
        buf = tile % 2

        @pl.when(first_ref[w] == 1)
        def _():
            pltpu.make_async_copy(x1_ref.at[pl.ds(0, MOE_TM // SUBLANES)], xs_ref.at[buf], sem.at[buf]).wait()

            @pl.when(tile + 1 < N_MOE_TILES)
            def _():
                start_gather(tile + 1, 1 - buf)

        @pl.when(newexp_ref[w] == 1)
        def _():
            wgu_b[...] = wgu_ref[...].astype(BF16)
            wdn_b[...] = wdn_ref[...].astype(BF16)

        xs = xs_ref[buf].reshape(MOE_TM, D_MODEL)
        h = jnp.dot(xs.astype(BF16), wgu_b[...], preferred_element_type=F32)
        act = _silu(h[:, :D_FF_EXPERT]) * h[:, D_FF_EXPERT:]
        y = jnp.dot(act.astype(BF16), wdn_b[...], preferred_element_type=F32)
        rows = tile_ref[w] * MOE_TM + lax.broadcasted_iota(jnp.int32, (MOE_TM, 1), 0)
        mine = (rows >= gstart_ref[w]) & (rows < gend_ref[w])

        @pl.when(first_ref[w] == 1)
        def _():
            o_ref[...] = jnp.where(mine, y, 0.0)

        @pl.when(first_ref[w] == 0)
        def _():
            o_ref[...] = jnp.where(mine, y, o_ref[...])


def _experts(items, slot_tokens, x1, w_gu, w_dn, layer):
    return pl.pallas_call(
        _experts_kernel,
        grid_spec=pltpu.PrefetchScalarGridSpec(
            num_scalar_prefetch=8,
            grid=(N_MOE_ITEMS,),
            in_specs=[pl.BlockSpec(memory_space=pl.ANY),
                      pl.BlockSpec((None, None, D_MODEL, 2 * D_FF_EXPERT),
                                   lambda w, t, e, *_: (layer, e[w], 0, 0)),
                      pl.BlockSpec((None, None, D_FF_EXPERT, D_MODEL),
                                   lambda w, t, e, *_: (layer, e[w], 0, 0))],
            out_specs=pl.BlockSpec((MOE_TM, D_MODEL), lambda w, t, e, *_: (t[w], 0)),
            scratch_shapes=[pltpu.VMEM((D_MODEL, 2 * D_FF_EXPERT), BF16),
                            pltpu.VMEM((D_FF_EXPERT, D_MODEL), BF16),
                            pltpu.VMEM((2, MOE_TM // SUBLANES, SUBLANES, D_MODEL), F32),
                            pltpu.SemaphoreType.DMA((2,))]),
        out_shape=jax.ShapeDtypeStruct((N_ASSIGN, D_MODEL), F32),
        compiler_params=_params("arbitrary"),
        name="moe_experts",
    )(*items, slot_tokens, x1, w_gu, w_dn)


def _combine_kernel(pos_ref, ys_ref, x1_ref, gate_ref, g_ref, b_ref, out_a_ref, out_b_ref, rows_ref, sem, *,
                    last_layer):
    i = pl.program_id(0)
    buf = i % 2

    def start_gather(tile, into):
        base = tile * (2 * ROW_TILE)

        def issue(g, carry):
            for j in range(SUBLANES):
                for pick in range(2):
                    slot = pos_ref[base + 2 * (g * SUBLANES + j) + pick]
                    pltpu.make_async_copy(ys_ref.at[slot >> 3, pl.ds(slot & (SUBLANES - 1), 1), :],
                                          rows_ref.at[into, pick, g, pl.ds(j, 1), :],
                                          sem.at[into]).start(priority=pick)
            return carry

        lax.fori_loop(0, ROW_TILE // SUBLANES, issue, 0)

    @pl.when(i == 0)
    def _():
        start_gather(0, 0)

    @pl.when(i + 1 < pl.num_programs(0))
    def _():
        start_gather(i + 1, 1 - buf)

    for pick in range(2):
        pltpu.make_async_copy(ys_ref.at[pl.ds(0, ROW_TILE // SUBLANES)], rows_ref.at[buf, pick], sem.at[buf]).wait()
    gate = gate_ref[...]
    y = (gate[:, 0:1] * rows_ref[buf, 0].reshape(ROW_TILE, D_MODEL)
         + gate[:, 1:2] * rows_ref[buf, 1].reshape(ROW_TILE, D_MODEL))
    x2 = _layer_norm(DEEPNORM_ALPHA * x1_ref[...] + y, g_ref[...], b_ref[...])
    if last_layer:
        is_prompt = i < N_PROMPT // ROW_TILE

        @pl.when(is_prompt)
        def _():
            out_a_ref[...] = x2

        @pl.when(jnp.logical_not(is_prompt))
        def _():
            out_b_ref[...] = x2
    else:
        out_a_ref[...] = x2
        out_b_ref[...] = x2.astype(BF16)


def _combine(pos_flat, ys, x1, gates, g, b, layer):
    row = lambda w: pl.BlockSpec((ROW_TILE, w), lambda i, pos: (i, 0))
    vec = _layer_slab((1, D_MODEL), layer)
    last_layer = layer == DEPTH - 1
    if last_layer:
        assert N_SAMPLE == ROW_TILE
        last_prompt = N_PROMPT // ROW_TILE - 1
        out_specs = [pl.BlockSpec((ROW_TILE, D_MODEL), lambda i, pos: (jnp.minimum(i, last_prompt), 0)),
                     pl.BlockSpec((ROW_TILE, D_MODEL), lambda i, pos: (0, 0))]
        out_shape = [jax.ShapeDtypeStruct((N_PROMPT, D_MODEL), F32), jax.ShapeDtypeStruct((N_SAMPLE, D_MODEL), F32)]
    else:
        out_specs = [row(D_MODEL), row(D_MODEL)]
        out_shape = [jax.ShapeDtypeStruct((N_TOK, D_MODEL), F32), jax.ShapeDtypeStruct((N_TOK, D_MODEL), BF16)]
    return pl.pallas_call(
        functools.partial(_combine_kernel, last_layer=last_layer),
        grid_spec=pltpu.PrefetchScalarGridSpec(
            num_scalar_prefetch=1,
            grid=(N_TOK // ROW_TILE,),
            in_specs=[pl.BlockSpec(memory_space=pl.ANY), row(D_MODEL), row(LANES), vec, vec],
            out_specs=out_specs,
            scratch_shapes=[pltpu.VMEM((2, 2, ROW_TILE // SUBLANES, SUBLANES, D_MODEL), F32),
                            pltpu.SemaphoreType.DMA((2,))]),
        out_shape=out_shape,
        compiler_params=_params("arbitrary"),
        name="moe_combine",
    )(pos_flat, ys, x1, gates, g, b)


def _router_weights(wg, bg, we, be):
    pad = LANES - N_GROUPS - N_EXPERTS
    wr = jnp.concatenate([wg, we, jnp.zeros((DEPTH, D_MODEL, pad), wg.dtype)], axis=-1)
    br = jnp.concatenate([bg, be, jnp.zeros((DEPTH, pad), bg.dtype)], axis=-1)
    return wr.astype(BF16), br.astype(F32).reshape(DEPTH, 1, LANES)


def kernel(x_prompt, x_sample, cache_swa_k, cache_swa_v, state_conv, state_delta, ln_in_g, ln_in_b, w_in, conv_w,
           a_log, dt_bias, dn_norm_w, attn_sinks, w_out, ln1_g, ln1_b, router_group_w, router_group_b,
           router_expert_w, router_expert_b, w_gate_up, w_down, ln2_g, ln2_b):
    x, xb = _ln_in(x_prompt.reshape(N_PROMPT, D_MODEL), x_sample.reshape(N_SAMPLE, D_MODEL), ln_in_g, ln_in_b)
    k_buf = cache_swa_k.reshape(DEPTH, DEC_BATCH, WINDOW, KV_WIDTH)
    v_buf = cache_swa_v.reshape(DEPTH, DEC_BATCH, WINDOW, KV_WIDTH)
    conv_state8 = jnp.pad(state_conv, ((0, 0), (0, 0), (8 - (CONV_W - 1), 0), (0, 0)))
    gp = _gdn_gate_params(a_log, dt_bias)
    norm_w = dn_norm_w.reshape(DEPTH, 1, DN_DV).astype(F32)
    wo_b = w_out.astype(BF16)
    w_in_t = jnp.swapaxes(w_in, 1, 2)
    wr_b, br = _router_weights(router_group_w, router_group_b, router_expert_w, router_expert_b)
    ln1_g, ln1_b, ln2_g, ln2_b = (a.reshape(DEPTH, 1, D_MODEL) for a in (ln1_g, ln1_b, ln2_g, ln2_b))
    kp, vp, cp, sp = [], [], [], []
    kq, vq, cq, sq = [], [], [], []
    for l in range(DEPTH):
        proj = _proj(xb, w_in_t, l)
        sinks = attn_sinks[l].astype(F32)
        o_attn_p = _attn_prompt(proj, sinks)
        o_attn_s, k_s, v_s = _attn_sample(proj, sinks, k_buf, v_buf, l)
        o_dn_p, s_p = _gdn_prompt(proj, conv_w, gp, norm_w, l)
        o_dn_s, s_s = _gdn_sample(proj, conv_w, gp, norm_w, conv_state8, state_delta, GDN_CHUNK_SAMPLE, l)
        x1, eid, gates, part_counts = _out_router(o_attn_p, o_attn_s, o_dn_p, o_dn_s, x, wo_b, ln1_g, ln1_b,
                                                  wr_b, br, l)
        pos, counts = _plan(eid, part_counts)
        pos_flat = pos[:, :2].reshape(N_ASSIGN)
        items = _work_items(counts[0, :N_EXPERTS])
        ys = _experts(items, _slot_tokens(pos_flat), x1.reshape(N_TOK // SUBLANES, SUBLANES, D_MODEL),
                      w_gate_up, w_down, l)
        x, xb = _combine(pos_flat, ys.reshape(N_ASSIGN // SUBLANES, SUBLANES, D_MODEL), x1, gates, ln2_g, ln2_b, l)

        def prompt_tail(rows, c0, c1):
            return jnp.stack([proj[(b + 1) * SEQ - rows:(b + 1) * SEQ, c0:c1] for b in range(BATCH)])

        kp.append(prompt_tail(WINDOW, COL_K, COL_V).reshape(BATCH, WINDOW, KV_HEADS, HEAD_DIM))
        vp.append(prompt_tail(WINDOW, COL_V, COL_QKV).reshape(BATCH, WINDOW, KV_HEADS, HEAD_DIM))
        cp.append(prompt_tail(CONV_W - 1, COL_QKV, COL_Z))
        sp.append(s_p)
        kq.append(k_s.reshape(DEC_BATCH, WINDOW, KV_HEADS, HEAD_DIM))
        vq.append(v_s.reshape(DEC_BATCH, WINDOW, KV_HEADS, HEAD_DIM))
        cq.append(proj[N_PROMPT:, COL_QKV:COL_Z].reshape(DEC_BATCH, DEC_SEQ, CONV_CH)[:, DEC_SEQ - (CONV_W - 1):])
        sq.append(s_s)
    return (x.reshape(BATCH, SEQ, D_MODEL), xb.reshape(DEC_BATCH, DEC_SEQ, D_MODEL),
            jnp.stack(kp), jnp.stack(vp), jnp.stack(cp), jnp.stack(sp),
            jnp.stack(kq), jnp.stack(vq), jnp.stack(cq), jnp.stack(sq))
```
